```python
import math
import jax, jax.numpy as jnp
from jax import lax
import numpy as np

D_MODEL = 1024
BATCH = 4
SEQ = 4096
DEPTH = 1

NORM_EPS = 1e-6
RW_HEADS = 8
RW_HEAD_DIM = 64
RW_WIDTH = RW_HEADS * RW_HEAD_DIM
DECAY_LORA = 64
AAA_LORA = 64
GATE_LORA = 160
RW_GN_EPS = 64e-5
RW_SHIFT_WIDTH = 3 * RW_WIDTH + DECAY_LORA + AAA_LORA + GATE_LORA
MLA_HEADS = 8
QK_NOPE = 64
QK_ROPE = 32
V_DIM = 64
Q_LORA = 384
KV_LORA = 256
ROPE_BASE = 10000.0
Q_BLOCK = 128
ATTN_SCALE = 1.0 / math.sqrt(QK_NOPE + QK_ROPE)
IN_WIDTH = RW_SHIFT_WIDTH + Q_LORA + KV_LORA + QK_ROPE + 2 * D_MODEL
PEER_HEADS = 8
N_KEYS = 128
N_EXPERTS = N_KEYS * N_KEYS
PEER_TOPK = 16
D_QUERY = 256
TOKEN_BLOCK = 128

kernel_name = "hybrid_rwkv7_mla_peer_block"


def rms_norm(x, w, eps=NORM_EPS):
    xf = x.astype(jnp.float32)
    y = xf * lax.rsqrt(jnp.mean(xf * xf, axis=-1, keepdims=True) + eps)
    return (y * w.astype(jnp.float32)).astype(x.dtype)


def apply_rope(x, pos):
    half = x.shape[-1] // 2
    inv_freq = ROPE_BASE ** (-jnp.arange(half, dtype=jnp.float32) / half)
    ang = pos.astype(jnp.float32)[..., None] * inv_freq
    ang = ang.reshape(ang.shape[:2] + (1,) * (x.ndim - 3) + (half,))
    cos, sin = jnp.cos(ang), jnp.sin(ang)
    xf = x.astype(jnp.float32)
    x1, x2 = xf[..., :half], xf[..., half:]
    return jnp.concatenate([x1 * cos - x2 * sin, x2 * cos + x1 * sin], axis=-1).astype(x.dtype)


def rwkv7_branch(p_rw, rw_mu, rw_w0, rw_w2, rw_a0, rw_a2, rw_g2, rw_k_k, rw_k_a, rw_r_k,
                 rw_ln_w, rw_ln_b, rw_w_o):
    B, S, _ = p_rw.shape
    p_prev = jnp.pad(p_rw, ((0, 0), (1, 0), (0, 0)))[:, :-1]
    p_mix = p_rw + rw_mu * (p_prev - p_rw)
    offs = np.cumsum([RW_WIDTH, RW_WIDTH, RW_WIDTH, DECAY_LORA, AAA_LORA]).tolist()
    r, k, v, wd, ad, gd = jnp.split(p_mix, offs, axis=-1)
    w = -jax.nn.softplus(-(rw_w0 + jnp.tanh(wd) @ rw_w2)) - 0.5
    decay = jnp.exp(-jnp.exp(w.astype(jnp.float32)))
    a = jax.nn.sigmoid(rw_a0 + ad @ rw_a2)
    g = jax.nn.sigmoid(gd) @ rw_g2
    kk = (k * rw_k_k).reshape(B, S, RW_HEADS, RW_HEAD_DIM).astype(jnp.float32)
    kk = kk / jnp.maximum(jnp.linalg.norm(kk, axis=-1, keepdims=True), 1e-12)
    k = k * (1.0 + (a - 1.0) * rw_k_a)
    hs = lambda t: t.reshape(B, S, RW_HEADS, RW_HEAD_DIM).astype(jnp.float32)
    r_h, k_h, v_h, a_h, w_h = hs(r), hs(k), hs(v), hs(a), hs(decay)

    def step(state, inp):
        r_t, w_t, k_t, v_t, kk_t, a_t = inp
        sa = jnp.einsum("bhvk,bhk->bhv", state, -kk_t)
        state = (state * w_t[:, :, None, :]
                 + sa[..., None] * (kk_t * a_t)[:, :, None, :]
                 + v_t[..., None] * k_t[:, :, None, :])
        return state, jnp.einsum("bhvk,bhk->bhv", state, r_t)

    xs = tuple(jnp.moveaxis(t, 1, 0) for t in (r_h, w_h, k_h, v_h, kk, a_h))
    s0 = jnp.zeros((B, RW_HEADS, RW_HEAD_DIM, RW_HEAD_DIM), jnp.float32)
    _, ys = lax.scan(step, s0, xs)
    y = jnp.moveaxis(ys, 0, 1)
    mu = jnp.mean(y, axis=-1, keepdims=True)
    var = jnp.mean(jnp.square(y - mu), axis=-1, keepdims=True)
    y = ((y - mu) * lax.rsqrt(var + RW_GN_EPS)).reshape(B, S, RW_WIDTH)
    y = y * rw_ln_w + rw_ln_b
    bonus = jnp.sum(r_h * k_h * rw_r_k.astype(jnp.float32), axis=-1, keepdims=True) * v_h
    y = (y + bonus.reshape(B, S, RW_WIDTH)).astype(p_rw.dtype) * g
    return y @ rw_w_o


def mla_branch(q_down, kv_down, k_pe_raw, positions, mla_q_norm_w, mla_w_uq, mla_kv_norm_w,
               mla_w_ukv, mla_qn_nope_w, mla_qn_pe_w, mla_kn_nope_w, mla_kn_pe_w, mla_w_o):
    B, S, _ = q_down.shape
    q = (rms_norm(q_down, mla_q_norm_w) @ mla_w_uq).reshape(B, S, MLA_HEADS, QK_NOPE + QK_ROPE)
    kv = (rms_norm(kv_down, mla_kv_norm_w) @ mla_w_ukv).reshape(B, S, MLA_HEADS, QK_NOPE + V_DIM)
    q_nope, q_pe = q[..., :QK_NOPE], q[..., QK_NOPE:]
    k_nope, v = kv[..., :QK_NOPE], kv[..., QK_NOPE:]
    q_nope = rms_norm(q_nope, mla_qn_nope_w)
    k_nope = rms_norm(k_nope, mla_kn_nope_w)
    q_pe = apply_rope(rms_norm(q_pe, mla_qn_pe_w), positions)
    k_pe = apply_rope(rms_norm(k_pe_raw, mla_kn_pe_w)[:, :, None, :], positions)
    q = jnp.concatenate([q_nope, q_pe], axis=-1)
    k = jnp.concatenate([k_nope, jnp.broadcast_to(k_pe, (B, S, MLA_HEADS, QK_ROPE))], axis=-1)
    nb = S // Q_BLOCK
    q_blocks = jnp.moveaxis(q.reshape(B, nb, Q_BLOCK, MLA_HEADS, QK_NOPE + QK_ROPE), 1, 0)
    k_pos = jnp.arange(S)

    def attend(args):
        q_blk, blk = args
        s = jnp.einsum("bqhd,bkhd->bhqk", q_blk, k).astype(jnp.float32) * ATTN_SCALE
        q_pos = blk * Q_BLOCK + jnp.arange(Q_BLOCK)
        s = jnp.where(k_pos[None, :] <= q_pos[:, None], s, jnp.finfo(jnp.float32).min)
        p = jax.nn.softmax(s, axis=-1).astype(v.dtype)
        return jnp.einsum("bhqk,bkhd->bqhd", p, v)

    o = lax.map(attend, (q_blocks, jnp.arange(nb)))
    o = jnp.moveaxis(o, 0, 1).reshape(B, S, MLA_HEADS * V_DIM)
    return o @ mla_w_o


def peer_ffn(hn, peer_w_query, peer_sub_keys, peer_u, peer_v):
    B, S, D = hn.shape
    T = B * S
    xt = hn.reshape(T, D)
    q = (xt @ peer_w_query).reshape(T, PEER_HEADS, 2, D_QUERY // 2)
    sc = jnp.einsum("thpd,hpnd->thpn", q, peer_sub_keys).astype(jnp.float32)
    s_top, i_top = lax.top_k(sc, PEER_TOPK)
    cand_s = (s_top[:, :, 0, :, None] + s_top[:, :, 1, None, :]).reshape(T, PEER_HEADS, PEER_TOPK * PEER_TOPK)
    cand_i = (i_top[:, :, 0, :, None] * N_KEYS + i_top[:, :, 1, None, :]).reshape(T, PEER_HEADS, PEER_TOPK * PEER_TOPK)
    best_s, best_pos = lax.top_k(cand_s, PEER_TOPK)
    expert_idx = jnp.take_along_axis(cand_i, best_pos, axis=-1)
    gates = jax.nn.softmax(best_s, axis=-1).astype(hn.dtype)
    nb = T // TOKEN_BLOCK
    xb = xt.reshape(nb, TOKEN_BLOCK, D)
    ib = expert_idx.reshape(nb, TOKEN_BLOCK, PEER_HEADS * PEER_TOPK)
    gb = gates.reshape(nb, TOKEN_BLOCK, PEER_HEADS * PEER_TOPK)

    def expert_block(args):
        x_blk, idx, g = args
        u = peer_u[idx]
        act = jax.nn.gelu(jnp.einsum("td,ted->te", x_blk, u), approximate=False)
        return jnp.einsum("te,ted->td", g * act, peer_v[idx])

    y = lax.map(expert_block, (xb, ib, gb))
    return y.reshape(B, S, D)


def setup_inputs(seed: int = 0) -> dict:
    key = jax.random.key(seed)
    ks = jax.random.split(key, 40)
    nrm = lambda i, shape, scale: jax.random.normal(ks[i], shape, jnp.float32) * scale
    gain = lambda i, n: 1.0 + 0.02 * jax.random.normal(ks[i], (n,), jnp.float32)
    return {
        "x": nrm(0, (BATCH, SEQ, D_MODEL), 1.0),
        "positions": (jnp.arange(SEQ, dtype=jnp.int32)[None, :]
                      + jax.random.randint(ks[1], (BATCH, 1), 0, 1024, jnp.int32)),
        "norm1_w": gain(2, D_MODEL),
        "w_in": nrm(3, (D_MODEL, IN_WIDTH), D_MODEL ** -0.5),
        "rw_mu": jax.random.uniform(ks[4], (RW_SHIFT_WIDTH,), jnp.float32),
        "rw_w0": jax.random.uniform(ks[5], (RW_WIDTH,), jnp.float32, -6.0, -1.0),
        "rw_w2": nrm(6, (DECAY_LORA, RW_WIDTH), 0.1),
        "rw_a0": nrm(7, (RW_WIDTH,), 0.1),
        "rw_a2": nrm(8, (AAA_LORA, RW_WIDTH), 0.1),
        "rw_g2": nrm(9, (GATE_LORA, RW_WIDTH), GATE_LORA ** -0.5),
        "rw_k_k": 0.85 + 0.05 * jax.random.normal(ks[10], (RW_WIDTH,), jnp.float32),
        "rw_k_a": 1.0 + 0.05 * jax.random.normal(ks[11], (RW_WIDTH,), jnp.float32),
        "rw_r_k": nrm(12, (RW_HEADS, RW_HEAD_DIM), 0.1),
        "rw_ln_w": gain(13, RW_WIDTH),
        "rw_ln_b": nrm(14, (RW_WIDTH,), 0.02),
        "rw_w_o": nrm(15, (RW_WIDTH, D_MODEL), RW_WIDTH ** -0.5),
        "mla_q_norm_w": gain(16, Q_LORA),
        "mla_w_uq": nrm(17, (Q_LORA, MLA_HEADS * (QK_NOPE + QK_ROPE)), Q_LORA ** -0.5),
        "mla_kv_norm_w": gain(18, KV_LORA),
        "mla_w_ukv": nrm(19, (KV_LORA, MLA_HEADS * (QK_NOPE + V_DIM)), KV_LORA ** -0.5),
        "mla_qn_nope_w": gain(20, QK_NOPE),
        "mla_qn_pe_w": gain(21, QK_ROPE),
        "mla_kn_nope_w": gain(22, QK_NOPE),
        "mla_kn_pe_w": gain(23, QK_ROPE),
        "mla_w_o": nrm(24, (MLA_HEADS * V_DIM, D_MODEL), (MLA_HEADS * V_DIM) ** -0.5),
        "w_out": nrm(25, (D_MODEL, D_MODEL), D_MODEL ** -0.5),
        "norm2_w": gain(26, D_MODEL),
        "peer_w_query": nrm(27, (D_MODEL, PEER_HEADS * D_QUERY), D_MODEL ** -0.5),
        "peer_sub_keys": nrm(28, (PEER_HEADS, 2, N_KEYS, D_QUERY // 2), (D_QUERY // 2) ** -0.5),
        "peer_u": nrm(29, (N_EXPERTS, D_MODEL), D_MODEL ** -0.5),
        "peer_v": nrm(30, (N_EXPERTS, D_MODEL), PEER_HEADS ** -0.5),
    }


def reference(x, positions, norm1_w, w_in, rw_mu, rw_w0, rw_w2, rw_a0, rw_a2, rw_g2, rw_k_k,
              rw_k_a, rw_r_k, rw_ln_w, rw_ln_b, rw_w_o, mla_q_norm_w, mla_w_uq, mla_kv_norm_w,
              mla_w_ukv, mla_qn_nope_w, mla_qn_pe_w, mla_kn_nope_w, mla_kn_pe_w, mla_w_o, w_out,
              norm2_w, peer_w_query, peer_sub_keys, peer_u, peer_v):
    h = x
    for _ in range(DEPTH):
        xn = rms_norm(h, norm1_w)
        p = xn @ w_in
        offs = np.cumsum([RW_SHIFT_WIDTH, Q_LORA, KV_LORA, QK_ROPE, D_MODEL]).tolist()
        p_rw, q_down, kv_down, k_pe_raw, g_a, g_b = jnp.split(p, offs, axis=-1)
        y_a = rwkv7_branch(p_rw, rw_mu, rw_w0, rw_w2, rw_a0, rw_a2, rw_g2, rw_k_k, rw_k_a,
                           rw_r_k, rw_ln_w, rw_ln_b, rw_w_o)
        y_b = mla_branch(q_down, kv_down, k_pe_raw, positions, mla_q_norm_w, mla_w_uq,
                         mla_kv_norm_w, mla_w_ukv, mla_qn_nope_w, mla_qn_pe_w, mla_kn_nope_w,
                         mla_kn_pe_w, mla_w_o)
        mixed = jax.nn.sigmoid(g_a) * y_a + jax.nn.sigmoid(g_b) * y_b
        h = h + mixed @ w_out
        h = h + peer_ffn(rms_norm(h, norm2_w), peer_w_query, peer_sub_keys, peer_u, peer_v)
    return h
```

```python
import functools
import math

import numpy as np
import jax
import jax.numpy as jnp
from jax import lax
from jax.experimental import pallas as pl
from jax.experimental.pallas import tpu as pltpu

F32 = jnp.float32
BF16 = jnp.bfloat16

D_MODEL = 1024
NORM_EPS = 1e-6
RW_HEADS = 8
RW_HEAD_DIM = 64
RW_WIDTH = RW_HEADS * RW_HEAD_DIM
DECAY_LORA = 64
AAA_LORA = 64
GATE_LORA = 160
RW_GN_EPS = 64e-5
RW_SHIFT_WIDTH = 3 * RW_WIDTH + DECAY_LORA + AAA_LORA + GATE_LORA
MLA_HEADS = 8
QK_NOPE = 64
QK_ROPE = 32
V_DIM = 64
Q_LORA = 384
KV_LORA = 256
ROPE_BASE = 10000.0
ATTN_SCALE = 1.0 / math.sqrt(QK_NOPE + QK_ROPE)
PEER_HEADS = 8
N_KEYS = 128
N_EXPERTS = N_KEYS * N_KEYS
PEER_TOPK = 16
D_QUERY = 256

LANES = 128
RW_OFF_WD = 3 * RW_WIDTH
RW_OFF_AD = RW_OFF_WD + LANES
RW_OFF_GD = RW_OFF_AD + LANES
GATE_PAD = 2 * LANES
RW_PAD = RW_OFF_GD + GATE_PAD
MLA_OFF_KV = Q_LORA
MLA_OFF_PE = Q_LORA + KV_LORA
MLA_PAD = MLA_OFF_PE + LANES
G_PAD = 2 * D_MODEL
IN_PAD = RW_PAD + MLA_PAD + G_PAD
HEAD_SLAB = LANES

RW_CHUNK = 64
VMEM_LIMIT = 56 * 1024 * 1024

NEG_INF = float("-inf")


def _cparams(*sem):
    return pltpu.CompilerParams(dimension_semantics=sem, vmem_limit_bytes=VMEM_LIMIT)


def _dot(a, b):
    return jnp.dot(a.astype(BF16), b.astype(BF16), preferred_element_type=F32)


def _dot_nt(a, b):
    return lax.dot_general(a.astype(BF16), b.astype(BF16), (((1,), (1,)), ((), ())),
                           preferred_element_type=F32)


def _dot_tn(a, b):
    return lax.dot_general(a.astype(BF16), b.astype(BF16), (((0,), (0,)), ((), ())),
                           preferred_element_type=F32)


def _dot_f32(a, b):
    return jnp.dot(a, b, precision=lax.Precision.HIGHEST, preferred_element_type=F32)


def _sigmoid(x):
    return 1.0 / (1.0 + jnp.exp(-x))


def _rms(x, w):
    return x * lax.rsqrt(jnp.mean(x * x, axis=-1, keepdims=True) + NORM_EPS) * w


def _inproj_kernel(x_ref, nw_ref, w_ref, orw_ref, omla_ref, og_ref):
    xb = _rms(x_ref[...], nw_ref[...]).astype(BF16)
    orw_ref[...] = jnp.dot(xb, w_ref[:, :RW_PAD], preferred_element_type=F32)
    omla_ref[...] = jnp.dot(xb, w_ref[:, RW_PAD:RW_PAD + MLA_PAD], preferred_element_type=F32)
    og_ref[...] = jnp.dot(xb, w_ref[:, RW_PAD + MLA_PAD:], preferred_element_type=F32).astype(BF16)


def _inproj(x2, norm1_w, w_cat, tm):
    T = x2.shape[0]
    row = lambda i: (i, 0)
    fixed = lambda i: (0, 0)
    return pl.pallas_call(
        _inproj_kernel,
        grid=(T // tm,),
        in_specs=[pl.BlockSpec((tm, D_MODEL), row),
                  pl.BlockSpec((1, D_MODEL), fixed),
                  pl.BlockSpec((D_MODEL, IN_PAD), fixed)],
        out_specs=[pl.BlockSpec((tm, RW_PAD), row),
                   pl.BlockSpec((tm, MLA_PAD), row),
                   pl.BlockSpec((tm, G_PAD), row)],
        out_shape=[jax.ShapeDtypeStruct((T, RW_PAD), F32),
                   jax.ShapeDtypeStruct((T, MLA_PAD), F32),
                   jax.ShapeDtypeStruct((T, G_PAD), BF16)],
        compiler_params=_cparams("parallel"),
        name="inproj",
    )(x2, norm1_w, w_cat)


def _rwprep_kernel(p_ref, pp_ref, mu_ref, w0_ref, w2_ref, a0_ref, a2_ref, g2_ref, kkw_ref,
                   kaw_ref, ones_ref, r_o, lw_o, k_o, v_o, kk_o, a_o, g_o, *, tiles_per_seq):
    i = pl.program_id(0)
    p = p_ref[...]
    prev_row = jnp.where(i % tiles_per_seq == 0, 0.0, pp_ref[7:8, :])
    rolled = pltpu.roll(p, 1, axis=0)
    row = lax.broadcasted_iota(jnp.int32, p.shape, 0)
    p_prev = jnp.where(row == 0, prev_row, rolled)
    pm = p + mu_ref[...] * (p_prev - p)
    r = pm[:, 0:RW_WIDTH]
    k = pm[:, RW_WIDTH:2 * RW_WIDTH]
    v = pm[:, 2 * RW_WIDTH:3 * RW_WIDTH]
    wd = pm[:, RW_OFF_WD:RW_OFF_AD]
    ad = pm[:, RW_OFF_AD:RW_OFF_GD]
    gd = pm[:, RW_OFF_GD:RW_PAD]
    z = -(w0_ref[...] + _dot_f32(jnp.tanh(wd), w2_ref[...]))
    softplus = jnp.maximum(z, 0.0) + jnp.log1p(jnp.exp(-jnp.abs(z)))
    lw_o[...] = -jnp.exp(-softplus - 0.5)
    a = _sigmoid(a0_ref[...] + _dot_f32(ad, a2_ref[...]))
    g_o[...] = _dot_f32(_sigmoid(gd), g2_ref[...])
    kk0 = k * kkw_ref[...]
    ss = _dot_f32(kk0 * kk0, ones_ref[...])
    kk_o[...] = kk0 / jnp.maximum(jnp.sqrt(ss), 1e-12)
    k_o[...] = k * (1.0 + (a - 1.0) * kaw_ref[...])
    r_o[...] = r
    v_o[...] = v
    a_o[...] = a


def _rwprep(p_rw, mu, w0, w2, a0, a2, g2, kkw, kaw, ones_blk, tm, S):
    T = p_rw.shape[0]
    row = lambda i: (i, 0)
    fixed = lambda i: (0, 0)
    sub = tm // 8
    vec = lambda n: pl.BlockSpec((1, n), fixed)
    out = jax.ShapeDtypeStruct((T, RW_WIDTH), F32)
    return pl.pallas_call(
        functools.partial(_rwprep_kernel, tiles_per_seq=S // tm),
        grid=(T // tm,),
        in_specs=[pl.BlockSpec((tm, RW_PAD), row),
                  pl.BlockSpec((8, RW_PAD), lambda i: (jnp.maximum(i * sub - 1, 0), 0)),
                  vec(RW_PAD), vec(RW_WIDTH),
                  pl.BlockSpec((LANES, RW_WIDTH), fixed),
                  vec(RW_WIDTH),
                  pl.BlockSpec((LANES, RW_WIDTH), fixed),
                  pl.BlockSpec((GATE_PAD, RW_WIDTH), fixed),
                  vec(RW_WIDTH), vec(RW_WIDTH),
                  pl.BlockSpec((RW_WIDTH, RW_WIDTH), fixed)],
        out_specs=[pl.BlockSpec((tm, RW_WIDTH), row)] * 7,
        out_shape=[out] * 7,
        compiler_params=_cparams("parallel"),
        name="rwprep",
    )(p_rw, p_rw, mu, w0, w2, a0, a2, g2, kkw, kaw, ones_blk)


def _scan_kernel(r_ref, lw_ref, k_ref, v_ref, kk_ref, a_ref, rk_ref, lnw_ref, lnb_ref, tri_ref,
                 o_ref, s_ref):
    c = pl.program_id(1)

    @pl.when(c == 0)
    def _():
        s_ref[...] = jnp.zeros_like(s_ref)

    C = r_ref.shape[0]
    N = RW_HEAD_DIM
    lw = lw_ref[...]
    cum = _dot_f32(tri_ref[...], lw)
    e_pos = jnp.exp(cum)
    e_prev = jnp.exp(cum - lw)
    e_neg = jnp.exp(-cum)
    kk = kk_ref[...]
    r_all = r_ref[...]
    k_all = k_ref[...]
    v_all = v_ref[...]
    A_all = -kk * e_prev
    B_all = kk * a_ref[...] * e_neg
    K_all = k_all * e_neg
    R_all = r_all * e_pos
    pc_all = e_pos[C - 1:C, :]

    row = lax.broadcasted_iota(jnp.int32, (C, C), 0)
    col = lax.broadcasted_iota(jnp.int32, (C, C), 1)
    strict = col < row
    incl = col <= row
    blk16 = (row // 16) == (col // 16)
    blk32 = (row // 32) == (col // 32)
    eye = (row == col).astype(F32)

    for h in range(RW_HEADS):
        sl = slice(h * N, (h + 1) * N)
        A, Bm, Km, R, V = A_all[:, sl], B_all[:, sl], K_all[:, sl], R_all[:, sl], v_all[:, sl]
        S0 = s_ref[h]
        AR = jnp.concatenate([A, R], axis=0)
        QB = _dot_nt(AR, Bm)
        QK = _dot_nt(AR, Km)
        Lb = jnp.where(strict, QB[:C], 0.0)
        Lk = jnp.where(strict, QK[:C], 0.0)
        Mb = jnp.where(incl, QB[C:], 0.0)
        Mk = jnp.where(incl, QK[C:], 0.0)
        Dg = jnp.where(blk16, Lb, 0.0)
        E1 = jnp.where(blk32 & ~blk16, Lb, 0.0)
        E2 = jnp.where(blk32, 0.0, Lb)
        D2 = _dot(Dg, Dg)
        D4 = _dot(D2, D2)
        D8 = _dot(D4, D4)
        X = eye + Dg
        X = X + _dot(X, D2)
        X = X + _dot(X, D4)
        X = X + _dot(X, D8)
        X = X + _dot(_dot(X, E1), X)
        Tm = X + _dot(_dot(X, E2), X)
        W1 = _dot(Tm, A)
        W2 = _dot(Tm, _dot(Lk, V))
        UR = _dot_nt(jnp.concatenate([W1, R], axis=0), S0)
        U = UR[:C] + W2
        O = UR[C:] + _dot(Mb, U) + _dot(Mk, V)
        UV = jnp.concatenate([U, V], axis=0)
        BK = jnp.concatenate([Bm, Km], axis=0)
        s_ref[h] = (S0 + _dot_tn(UV, BK)) * pc_all[:, sl]
        mu = jnp.mean(O, axis=-1, keepdims=True)
        var = jnp.mean(jnp.square(O - mu), axis=-1, keepdims=True)
        y = (O - mu) * lax.rsqrt(var + RW_GN_EPS) * lnw_ref[:, sl] + lnb_ref[:, sl]
        bonus = jnp.sum(r_all[:, sl] * k_all[:, sl] * rk_ref[:, sl], axis=-1, keepdims=True) * V
        o_ref[:, sl] = y + bonus


def _rwscan(r, lw, k, v, kk, a, rk, lnw, lnb, tri, B, S):
    C = RW_CHUNK
    nc = S // C
    blk = pl.BlockSpec((C, RW_WIDTH), lambda b, c: (b * nc + c, 0))
    vec = pl.BlockSpec((1, RW_WIDTH), lambda b, c: (0, 0))
    return pl.pallas_call(
        _scan_kernel,
        grid=(B, nc),
        in_specs=[blk] * 6 + [vec, vec, vec, pl.BlockSpec((C, C), lambda b, c: (0, 0))],
        out_specs=blk,
        out_shape=jax.ShapeDtypeStruct((B * S, RW_WIDTH), F32),
        scratch_shapes=[pltpu.VMEM((RW_HEADS, RW_HEAD_DIM, RW_HEAD_DIM), F32)],
        compiler_params=_cparams("parallel", "arbitrary"),
        name="rwscan",
    )(r, lw, k, v, kk, a, rk, lnw, lnb, tri)


def _mlaprep_kernel(p_ref, pos_ref, qnw_ref, wuq_ref, kvnw_ref, wuk_ref, wuv_ref, seg_ref,
                    qlw_ref, klw_ref, pew_ref, freq_ref, q_o, k_o, v_o):
    p = p_ref[...]
    qn = _rms(p[:, :Q_LORA], qnw_ref[...]).astype(BF16)
    kvn = _rms(p[:, MLA_OFF_KV:MLA_OFF_PE], kvnw_ref[...]).astype(BF16)
    q_all = jnp.dot(qn, wuq_ref[...], preferred_element_type=F32)
    k_all = jnp.dot(kvn, wuk_ref[...], preferred_element_type=F32)
    v_all = jnp.dot(kvn, wuv_ref[...], preferred_element_type=F32)
    seg = seg_ref[...]
    ang = pos_ref[...].astype(F32) * freq_ref[...]
    cos = jnp.cos(ang)
    sin = jnp.sin(ang)
    lane = lax.broadcasted_iota(jnp.int32, ang.shape, 1)
    first_half = lane < QK_NOPE + QK_ROPE // 2

    def seg_norm(x, w):
        return x * lax.rsqrt(_dot_f32(x * x, seg) + NORM_EPS) * w

    def rope(x):
        partner = jnp.where(first_half, -pltpu.roll(x, LANES - QK_ROPE // 2, axis=1),
                            pltpu.roll(x, QK_ROPE // 2, axis=1))
        return x * cos + partner * sin

    kpe = rope(seg_norm(p[:, MLA_OFF_PE:MLA_PAD], pew_ref[...]))
    for h in range(MLA_HEADS):
        sl = slice(h * HEAD_SLAB, (h + 1) * HEAD_SLAB)
        q_o[0, h] = (rope(seg_norm(q_all[:, sl], qlw_ref[...])) * ATTN_SCALE).astype(BF16)
        k_o[0, h] = (seg_norm(k_all[:, sl], klw_ref[...]) + kpe).astype(BF16)
        v_o[0, h] = v_all[:, sl].astype(BF16)


def _mlaprep(p_mla, pos, qnw, wuq, kvnw, wuk, wuv, seg, qlw, klw, pew, freq, tm, B, S):
    ns = S // tm
    W = MLA_HEADS * HEAD_SLAB
    row = lambda b, s: (b * ns + s, 0)
    fixed = lambda b, s: (0, 0)
    vec = lambda n: pl.BlockSpec((1, n), fixed)
    oblk = pl.BlockSpec((1, MLA_HEADS, tm, HEAD_SLAB), lambda b, s: (b, 0, s, 0))
    out = jax.ShapeDtypeStruct((B, MLA_HEADS, S, HEAD_SLAB), BF16)
    return pl.pallas_call(
        _mlaprep_kernel,
        grid=(B, ns),
        in_specs=[pl.BlockSpec((tm, MLA_PAD), row),
                  pl.BlockSpec((tm, 1), row),
                  vec(Q_LORA), pl.BlockSpec((Q_LORA, W), fixed),
                  vec(KV_LORA), pl.BlockSpec((KV_LORA, W), fixed), pl.BlockSpec((KV_LORA, W), fixed),
                  pl.BlockSpec((LANES, LANES), fixed),
                  vec(LANES), vec(LANES), vec(LANES), vec(LANES)],
        out_specs=[oblk] * 3,
        out_shape=[out] * 3,
        compiler_params=_cparams("parallel", "parallel"),
        name="mlaprep",
    )(p_mla, pos, qnw, wuq, kvnw, wuk, wuv, seg, qlw, klw, pew, freq)


def _attn_kernel(q_ref, k_ref, v_ref, o_ref, m_ref, l_ref, acc_ref, *, blk):
    qi = pl.program_id(2)
    ki = pl.program_id(3)

    @pl.when(ki == 0)
    def _():
        m_ref[...] = jnp.full_like(m_ref, NEG_INF)
        l_ref[...] = jnp.zeros_like(l_ref)
        acc_ref[...] = jnp.zeros_like(acc_ref)

    @pl.when(ki <= qi)
    def _():
        s = lax.dot_general(q_ref[0, 0], k_ref[0, 0], (((1,), (1,)), ((), ())),
                            preferred_element_type=F32)
        row = qi * blk + lax.broadcasted_iota(jnp.int32, s.shape, 0)
        col = ki * blk + lax.broadcasted_iota(jnp.int32, s.shape, 1)
        s = jnp.where(col <= row, s, jnp.finfo(F32).min)
        m_prev = m_ref[...]
        m_new = jnp.maximum(m_prev, jnp.max(s, axis=-1, keepdims=True))
        alpha = jnp.exp(m_prev - m_new)
        p = jnp.exp(s - m_new)
        l_ref[...] = alpha * l_ref[...] + jnp.sum(p, axis=-1, keepdims=True)
        acc_ref[...] = alpha * acc_ref[...] + jnp.dot(p.astype(BF16), v_ref[0, 0],
                                                      preferred_element_type=F32)
        m_ref[...] = m_new

    @pl.when(ki == qi)
    def _():
        o_ref[0] = (acc_ref[...] / l_ref[...]).astype(BF16)


def _attention(q, k, v, blk):
    B, H, S, W = q.shape
    n = S // blk
    qspec = pl.BlockSpec((1, 1, blk, W), lambda b, h, i, j: (b, h, i, 0))
    kspec = pl.BlockSpec((1, 1, blk, W), lambda b, h, i, j: (b, h, jnp.minimum(i, j), 0))
    return pl.pallas_call(
        functools.partial(_attn_kernel, blk=blk),
        grid=(B, H, n, n),
        in_specs=[qspec, kspec, kspec],
        out_specs=pl.BlockSpec((1, blk, W), lambda b, h, i, j: (b, i, h)),
        out_shape=jax.ShapeDtypeStruct((B, S, H * W), BF16),
        scratch_shapes=[pltpu.VMEM((blk, 1), F32), pltpu.VMEM((blk, 1), F32),
                        pltpu.VMEM((blk, W), F32)],
        compiler_params=_cparams("parallel", "parallel", "parallel", "arbitrary"),
        name="attention",
    )(q, k, v)


def _merge_kernel(x_ref, y_ref, grw_ref, o_ref, g_ref, wro_ref, wmo_ref, wout_ref, n2_ref, wq_ref,
                  h_o, hn_o, qp_o):
    ya = _dot(y_ref[...] * grw_ref[...], wro_ref[...])
    yb = jnp.dot(o_ref[...], wmo_ref[...], preferred_element_type=F32)
    g = g_ref[...].astype(F32)
    mixed = _sigmoid(g[:, :D_MODEL]) * ya + _sigmoid(g[:, D_MODEL:]) * yb
    h = x_ref[...] + _dot(mixed, wout_ref[...])
    hn = _rms(h, n2_ref[...]).astype(BF16)
    h_o[...] = h
    hn_o[...] = hn
    qp = jnp.dot(hn, wq_ref[...], preferred_element_type=F32).astype(BF16)
    for j in range(2 * PEER_HEADS):
        qp_o[j] = qp[:, j * LANES:(j + 1) * LANES]


def _merge(x2, y_rw, g_rw, o_mla, g, wro, wmo, wout, n2, wq, tm):
    T = x2.shape[0]
    row = lambda i: (i, 0)
    fixed = lambda i: (0, 0)
    W = MLA_HEADS * HEAD_SLAB
    NQ = 2 * PEER_HEADS
    return pl.pallas_call(
        _merge_kernel,
        grid=(T // tm,),
        in_specs=[pl.BlockSpec((tm, D_MODEL), row),
                  pl.BlockSpec((tm, RW_WIDTH), row),
                  pl.BlockSpec((tm, RW_WIDTH), row),
                  pl.BlockSpec((tm, W), row),
                  pl.BlockSpec((tm, G_PAD), row),
                  pl.BlockSpec((RW_WIDTH, D_MODEL), fixed),
                  pl.BlockSpec((W, D_MODEL), fixed),
                  pl.BlockSpec((D_MODEL, D_MODEL), fixed),
                  pl.BlockSpec((1, D_MODEL), fixed),
                  pl.BlockSpec((D_MODEL, NQ * LANES), fixed)],
        out_specs=[pl.BlockSpec((tm, D_MODEL), row),
                   pl.BlockSpec((tm, D_MODEL), row),
                   pl.BlockSpec((NQ, tm, LANES), lambda i: (0, i, 0))],
        out_shape=[jax.ShapeDtypeStruct((T, D_MODEL), F32),
                   jax.ShapeDtypeStruct((T, D_MODEL), BF16),
                   jax.ShapeDtypeStruct((NQ, T, LANES), BF16)],
        compiler_params=_cparams("parallel"),
        name="merge",
    )(x2, y_rw, g_rw, o_mla, g, wro, wmo, wout, n2, wq)


def _peer_kernel(qp_ref, keys_ref, hn_ref, h_ref, u_ref, vt_ref, out_ref,
                 s2_s, nb_s, e1_s, e2_s, act_s, ga_s, yt_s, *, TB, ET):
    e = pl.program_id(1)
    K = PEER_TOPK

    def top_rows(sc):
        rows = []
        for _ in range(K + 1):
            m = jnp.max(sc, axis=0, keepdims=True)
            rows.append(m)
            sc = jnp.where(sc >= m, NEG_INF, sc)
        return rows

    @pl.when(e == 0)
    def _prep():
        yt_s[...] = jnp.zeros_like(yt_s)

        def head(h, carry):
            s1 = _dot_nt(keys_ref[2 * h], qp_ref[2 * h])
            s2 = _dot_nt(keys_ref[2 * h + 1], qp_ref[2 * h + 1])
            a1 = top_rows(s1)
            a2 = top_rows(s2)
            a2_lo = jnp.concatenate(a2[:8], axis=0)
            pieces = [a1[0] + jnp.concatenate(a2[:K], axis=0)]
            pieces += [a1[i] + a2_lo for i in range(1, 8)]
            pieces += [jnp.concatenate(a1[8:K], axis=0) + a2[0]]
            pieces += [a1[0] + a2[K], a1[K] + a2[0], jnp.full((6, TB), NEG_INF, F32)]
            cand = jnp.concatenate(pieces, axis=0)
            top = a1[0] + a2[0]
            z = jnp.zeros_like(top)
            m = top
            for _ in range(K):
                m = jnp.max(cand, axis=0, keepdims=True)
                z = z + jnp.exp(m - top)
                cand = jnp.where(cand >= m, NEG_INF, cand)
            tau = 0.5 * (m + jnp.max(cand, axis=0, keepdims=True))
            s2_s[h] = s2
            nb_s[h] = tau - s1
            e1_s[h] = jnp.exp(s1 - a1[0]) / z
            e2_s[h] = jnp.exp(s2 - a2[0])
            return carry

        lax.fori_loop(0, PEER_HEADS, head, 0)

    act_s[...] = lax.dot_general(u_ref[...], hn_ref[...], (((1,), (1,)), ((), ())),
                                 preferred_element_type=F32)
    n_i = ET // N_KEYS
    i0 = pl.multiple_of(e * n_i, n_i)
    for tc in range(TB // LANES):
        ts = slice(tc * LANES, (tc + 1) * LANES)
        nb_rows = [nb_s[h, pl.ds(i0, n_i), ts] for h in range(PEER_HEADS)]
        e1_rows = [e1_s[h, pl.ds(i0, n_i), ts] for h in range(PEER_HEADS)]
        for ii in range(n_i):
            G = jnp.zeros((N_KEYS, LANES), F32)
            for h in range(PEER_HEADS):
                nb_row = nb_rows[h][ii:ii + 1]
                e1_row = e1_rows[h][ii:ii + 1]
                G = G + jnp.where(s2_s[h, :, ts] >= nb_row, e2_s[h, :, ts], 0.0) * e1_row
            act = act_s[ii * N_KEYS:(ii + 1) * N_KEYS, ts]
            gelu = 0.5 * act * (1.0 + lax.erf(act * (1.0 / math.sqrt(2.0))))
            ga_s[ii * N_KEYS:(ii + 1) * N_KEYS, ts] = (G * gelu).astype(BF16)
    yt_s[...] += jnp.dot(vt_ref[...], ga_s[...], preferred_element_type=F32)

    @pl.when(e == pl.num_programs(1) - 1)
    def _():
        out_ref[...] = h_ref[...] + yt_s[...].T


def _peer(qp, keys, hn, h, u, vt, TB, ET):
    T = hn.shape[0]
    NQ = 2 * PEER_HEADS
    tok = lambda t, e: (t, 0)
    return pl.pallas_call(
        functools.partial(_peer_kernel, TB=TB, ET=ET),
        grid=(T // TB, N_EXPERTS // ET),
        in_specs=[pl.BlockSpec((NQ, TB, LANES), lambda t, e: (0, t, 0)),
                  pl.BlockSpec((NQ, N_KEYS, LANES), lambda t, e: (0, 0, 0)),
                  pl.BlockSpec((TB, D_MODEL), tok),
                  pl.BlockSpec((TB, D_MODEL), tok),
                  pl.BlockSpec((ET, D_MODEL), lambda t, e: (e, 0)),
                  pl.BlockSpec((D_MODEL, ET), lambda t, e: (0, e))],
        out_specs=pl.BlockSpec((TB, D_MODEL), tok),
        out_shape=jax.ShapeDtypeStruct((T, D_MODEL), F32),
        scratch_shapes=[pltpu.VMEM((PEER_HEADS, N_KEYS, TB), F32)] * 4
        + [pltpu.VMEM((ET, TB), F32), pltpu.VMEM((ET, TB), BF16), pltpu.VMEM((D_MODEL, TB), F32)],
        compiler_params=_cparams("parallel", "arbitrary"),
        name="peer",
    )(qp, keys, hn, h, u, vt)


def _pad_cols(w, n):
    return jnp.pad(w, ((0, 0), (0, n - w.shape[1])))


def _pad_rows(w, n):
    return jnp.pad(w, ((0, n - w.shape[0]), (0, 0)))


def _head_slabs(w, width):
    k = w.shape[0]
    w = w.reshape(k, MLA_HEADS, width)
    return jnp.pad(w, ((0, 0), (0, 0), (0, HEAD_SLAB - width))).reshape(k, MLA_HEADS * HEAD_SLAB)


def _lane_vec(nope, pe):
    v = jnp.zeros((LANES,), F32)
    if nope is not None:
        v = v.at[:QK_NOPE].set(nope)
    if pe is not None:
        v = v.at[QK_NOPE:QK_NOPE + QK_ROPE].set(pe)
    return v.reshape(1, LANES)


def kernel(x, positions, norm1_w, w_in, rw_mu, rw_w0, rw_w2, rw_a0, rw_a2, rw_g2, rw_k_k, rw_k_a,
           rw_r_k, rw_ln_w, rw_ln_b, rw_w_o, mla_q_norm_w, mla_w_uq, mla_kv_norm_w, mla_w_ukv,
           mla_qn_nope_w, mla_qn_pe_w, mla_kn_nope_w, mla_kn_pe_w, mla_w_o, w_out, norm2_w,
           peer_w_query, peer_sub_keys, peer_u, peer_v):
    B, S, D = x.shape
    T = B * S
    x2 = x.reshape(T, D)
    row = lambda v: v.reshape(1, -1).astype(F32)

    o_q = RW_SHIFT_WIDTH
    o_kv = o_q + Q_LORA
    o_pe = o_kv + KV_LORA
    o_g = o_pe + QK_ROPE
    seg3 = 3 * RW_WIDTH
    w_rw = jnp.concatenate([
        w_in[:, :seg3],
        _pad_cols(w_in[:, seg3:seg3 + DECAY_LORA], LANES),
        _pad_cols(w_in[:, seg3 + DECAY_LORA:seg3 + DECAY_LORA + AAA_LORA], LANES),
        _pad_cols(w_in[:, seg3 + DECAY_LORA + AAA_LORA:o_q], GATE_PAD)], axis=1)
    pe_slab = jnp.zeros((D, LANES), F32).at[:, QK_NOPE:QK_NOPE + QK_ROPE].set(w_in[:, o_pe:o_g])
    w_cat = jnp.concatenate([w_rw, w_in[:, o_q:o_pe], pe_slab, w_in[:, o_g:]], axis=1).astype(BF16)
    mu = jnp.concatenate([
        rw_mu[:seg3],
        jnp.pad(rw_mu[seg3:seg3 + DECAY_LORA], (0, LANES - DECAY_LORA)),
        jnp.pad(rw_mu[seg3 + DECAY_LORA:seg3 + DECAY_LORA + AAA_LORA], (0, LANES - AAA_LORA)),
        jnp.pad(rw_mu[seg3 + DECAY_LORA + AAA_LORA:], (0, GATE_PAD - GATE_LORA))]).reshape(1, RW_PAD)

    tm = min(256, S)
    p_rw, p_mla, g = _inproj(x2, row(norm1_w), w_cat, tm)

    hid = np.arange(RW_WIDTH) // RW_HEAD_DIM
    ones_blk = jnp.asarray((hid[:, None] == hid[None, :]).astype(np.float32))
    tp = min(512, S)
    r, lw, k2, v, kk, a, g_rw = _rwprep(
        p_rw, mu, row(rw_w0), _pad_rows(rw_w2, LANES), row(rw_a0), _pad_rows(rw_a2, LANES),
        _pad_rows(rw_g2, GATE_PAD), row(rw_k_k), row(rw_k_a), ones_blk, tp, S)
    tri = jnp.asarray(np.tril(np.ones((RW_CHUNK, RW_CHUNK), np.float32)))
    y_rw = _rwscan(r, lw, k2, v, kk, a, row(rw_r_k), row(rw_ln_w), row(rw_ln_b), tri, B, S)

    dq = QK_NOPE + QK_ROPE
    wuq = _head_slabs(mla_w_uq, dq).astype(BF16)
    ukv = mla_w_ukv.reshape(KV_LORA, MLA_HEADS, QK_NOPE + V_DIM)
    wuk = _head_slabs(ukv[:, :, :QK_NOPE].reshape(KV_LORA, -1), QK_NOPE).astype(BF16)
    wuv = _head_slabs(ukv[:, :, QK_NOPE:].reshape(KV_LORA, -1), V_DIM).astype(BF16)
    lane = np.arange(LANES)
    in_nope = lane < QK_NOPE
    in_pe = (lane >= QK_NOPE) & (lane < dq)
    seg = (in_nope[:, None] & in_nope[None, :]) / QK_NOPE + (in_pe[:, None] & in_pe[None, :]) / QK_ROPE
    half = QK_ROPE // 2
    inv_freq = ROPE_BASE ** (-jnp.arange(half, dtype=F32) / half)
    freq = jnp.zeros((LANES,), F32).at[QK_NOPE:dq].set(jnp.concatenate([inv_freq, inv_freq]))
    q, k, vv = _mlaprep(
        p_mla, positions.reshape(T, 1).astype(jnp.int32), row(mla_q_norm_w), wuq,
        row(mla_kv_norm_w), wuk, wuv, jnp.asarray(seg.astype(np.float32)),
        _lane_vec(mla_qn_nope_w, mla_qn_pe_w), _lane_vec(mla_kn_nope_w, None),
        _lane_vec(None, mla_kn_pe_w), freq.reshape(1, LANES), tp, B, S)
    o_mla = _attention(q, k, vv, min(512, S)).reshape(T, MLA_HEADS * HEAD_SLAB)

    wmo = jnp.pad(mla_w_o.reshape(MLA_HEADS, V_DIM, D), ((0, 0), (0, HEAD_SLAB - V_DIM), (0, 0)))
    wmo = wmo.reshape(MLA_HEADS * HEAD_SLAB, D).astype(BF16)
    h, hn, qp = _merge(x2, y_rw, g_rw, o_mla, g, rw_w_o.astype(BF16), wmo, w_out.astype(BF16),
                       row(norm2_w), peer_w_query.astype(BF16), tm)

    keys = peer_sub_keys.reshape(2 * PEER_HEADS, N_KEYS, D_QUERY // 2).astype(BF16)
    out = _peer(qp, keys, hn, h, peer_u.astype(BF16), peer_v.T.astype(BF16), min(512, T), 1024)
    return out.reshape(B, S, D)
```

```python
import functools
import math

import numpy as np
import jax
import jax.numpy as jnp
from jax import lax
from jax.experimental import pallas as pl
from jax.experimental.pallas import tpu as pltpu

F32 = jnp.float32
BF16 = jnp.bfloat16

D_MODEL = 1024
NORM_EPS = 1e-6
RW_HEADS = 8
RW_HEAD_DIM = 64
RW_WIDTH = RW_HEADS * RW_HEAD_DIM
DECAY_LORA = 64
AAA_LORA = 64
GATE_LORA = 160
RW_GN_EPS = 64e-5
RW_SHIFT_WIDTH = 3 * RW_WIDTH + DECAY_LORA + AAA_LORA + GATE_LORA
MLA_HEADS = 8
QK_NOPE = 64
QK_ROPE = 32
V_DIM = 64
Q_LORA = 384
KV_LORA = 256
ROPE_BASE = 10000.0
ATTN_SCALE = 1.0 / math.sqrt(QK_NOPE + QK_ROPE)
PEER_HEADS = 8
N_KEYS = 128
N_EXPERTS = N_KEYS * N_KEYS
PEER_TOPK = 16
D_QUERY = 256

LANES = 128
RW_OFF_WD = 3 * RW_WIDTH
RW_OFF_AD = RW_OFF_WD + LANES
RW_OFF_GD = RW_OFF_AD + LANES
GATE_PAD = 2 * LANES
RW_PAD = RW_OFF_GD + GATE_PAD
MLA_OFF_KV = Q_LORA
MLA_OFF_PE = Q_LORA + KV_LORA
MLA_PAD = MLA_OFF_PE + LANES
G_PAD = 2 * D_MODEL
IN_PAD = RW_PAD + MLA_PAD + G_PAD
HEAD_SLAB = LANES

RW_CHUNK = 64
VMEM_LIMIT = 56 * 1024 * 1024

NEG_INF = float("-inf")


def _cparams(*sem):
    return pltpu.CompilerParams(dimension_semantics=sem, vmem_limit_bytes=VMEM_LIMIT)


def _dot(a, b):
    return jnp.dot(a.astype(BF16), b.astype(BF16), preferred_element_type=F32)


def _dot_nt(a, b):
    return lax.dot_general(a.astype(BF16), b.astype(BF16), (((1,), (1,)), ((), ())),
                           preferred_element_type=F32)


def _dot_tn(a, b):
    return lax.dot_general(a.astype(BF16), b.astype(BF16), (((0,), (0,)), ((), ())),
                           preferred_element_type=F32)


def _dot_f32(a, b):
    return jnp.dot(a, b, precision=lax.Precision.HIGHEST, preferred_element_type=F32)


def _sigmoid(x):
    return 1.0 / (1.0 + jnp.exp(-x))


def _rms(x, w):
    return x * lax.rsqrt(jnp.mean(x * x, axis=-1, keepdims=True) + NORM_EPS) * w


def _inproj_kernel(x_ref, nw_ref, w_ref, orw_ref, omla_ref, og_ref):
    xb = _rms(x_ref[...], nw_ref[...]).astype(BF16)
    orw_ref[...] = jnp.dot(xb, w_ref[:, :RW_PAD], preferred_element_type=F32)
    omla_ref[...] = jnp.dot(xb, w_ref[:, RW_PAD:RW_PAD + MLA_PAD], preferred_element_type=F32)
    og_ref[...] = jnp.dot(xb, w_ref[:, RW_PAD + MLA_PAD:], preferred_element_type=F32).astype(BF16)


def _inproj(x2, norm1_w, w_cat, tm):
    T = x2.shape[0]
    row = lambda i: (i, 0)
    fixed = lambda i: (0, 0)
    return pl.pallas_call(
        _inproj_kernel,
        grid=(T // tm,),
        in_specs=[pl.BlockSpec((tm, D_MODEL), row),
                  pl.BlockSpec((1, D_MODEL), fixed),
                  pl.BlockSpec((D_MODEL, IN_PAD), fixed)],
        out_specs=[pl.BlockSpec((tm, RW_PAD), row),
                   pl.BlockSpec((tm, MLA_PAD), row),
                   pl.BlockSpec((tm, G_PAD), row)],
        out_shape=[jax.ShapeDtypeStruct((T, RW_PAD), F32),
                   jax.ShapeDtypeStruct((T, MLA_PAD), F32),
                   jax.ShapeDtypeStruct((T, G_PAD), BF16)],
        compiler_params=_cparams("parallel"),
        name="inproj",
    )(x2, norm1_w, w_cat)


def _rwprep_kernel(p_ref, pp_ref, mu_ref, w0_ref, w2_ref, a0_ref, a2_ref, g2_ref, kkw_ref,
                   kaw_ref, ones_ref, r_o, lw_o, k_o, v_o, kk_o, a_o, g_o, *, tiles_per_seq):
    i = pl.program_id(0)
    p = p_ref[...]
    prev_row = jnp.where(i % tiles_per_seq == 0, 0.0, pp_ref[7:8, :])
    rolled = pltpu.roll(p, 1, axis=0)
    row = lax.broadcasted_iota(jnp.int32, p.shape, 0)
    p_prev = jnp.where(row == 0, prev_row, rolled)
    pm = p + mu_ref[...] * (p_prev - p)
    r = pm[:, 0:RW_WIDTH]
    k = pm[:, RW_WIDTH:2 * RW_WIDTH]
    v = pm[:, 2 * RW_WIDTH:3 * RW_WIDTH]
    wd = pm[:, RW_OFF_WD:RW_OFF_AD]
    ad = pm[:, RW_OFF_AD:RW_OFF_GD]
    gd = pm[:, RW_OFF_GD:RW_PAD]
    z = -(w0_ref[...] + _dot_f32(jnp.tanh(wd), w2_ref[...]))
    softplus = jnp.maximum(z, 0.0) + jnp.log1p(jnp.exp(-jnp.abs(z)))
    lw_o[...] = -jnp.exp(-softplus - 0.5)
    a = _sigmoid(a0_ref[...] + _dot_f32(ad, a2_ref[...]))
    g_o[...] = _dot_f32(_sigmoid(gd), g2_ref[...])
    kk0 = k * kkw_ref[...]
    ss = _dot_f32(kk0 * kk0, ones_ref[...])
    kk_o[...] = kk0 / jnp.maximum(jnp.sqrt(ss), 1e-12)
    k_o[...] = k * (1.0 + (a - 1.0) * kaw_ref[...])
    r_o[...] = r
    v_o[...] = v
    a_o[...] = a


def _rwprep(p_rw, mu, w0, w2, a0, a2, g2, kkw, kaw, ones_blk, tm, S):
    T = p_rw.shape[0]
    row = lambda i: (i, 0)
    fixed = lambda i: (0, 0)
    sub = tm // 8
    vec = lambda n: pl.BlockSpec((1, n), fixed)
    out = jax.ShapeDtypeStruct((T, RW_WIDTH), F32)
    return pl.pallas_call(
        functools.partial(_rwprep_kernel, tiles_per_seq=S // tm),
        grid=(T // tm,),
        in_specs=[pl.BlockSpec((tm, RW_PAD), row),
                  pl.BlockSpec((8, RW_PAD), lambda i: (jnp.maximum(i * sub - 1, 0), 0)),
                  vec(RW_PAD), vec(RW_WIDTH),
                  pl.BlockSpec((LANES, RW_WIDTH), fixed),
                  vec(RW_WIDTH),
                  pl.BlockSpec((LANES, RW_WIDTH), fixed),
                  pl.BlockSpec((GATE_PAD, RW_WIDTH), fixed),
                  vec(RW_WIDTH), vec(RW_WIDTH),
                  pl.BlockSpec((RW_WIDTH, RW_WIDTH), fixed)],
        out_specs=[pl.BlockSpec((tm, RW_WIDTH), row)] * 7,
        out_shape=[out] * 7,
        compiler_params=_cparams("parallel"),
        name="rwprep",
    )(p_rw, p_rw, mu, w0, w2, a0, a2, g2, kkw, kaw, ones_blk)


def _scan_kernel(r_ref, lw_ref, k_ref, v_ref, kk_ref, a_ref, rk_ref, lnw_ref, lnb_ref, tri_ref,
                 o_ref, s_ref):
    c = pl.program_id(1)

    @pl.when(c == 0)
    def _():
        s_ref[...] = jnp.zeros_like(s_ref)

    C = r_ref.shape[0]
    N = RW_HEAD_DIM
    lw = lw_ref[...]
    cum = _dot_f32(tri_ref[...], lw)
    e_pos = jnp.exp(cum)
    e_prev = jnp.exp(cum - lw)
    e_neg = jnp.exp(-cum)
    kk = kk_ref[...]
    r_all = r_ref[...]
    k_all = k_ref[...]
    v_all = v_ref[...]
    A_all = -kk * e_prev
    B_all = kk * a_ref[...] * e_neg
    K_all = k_all * e_neg
    R_all = r_all * e_pos
    pc_all = e_pos[C - 1:C, :]

    row = lax.broadcasted_iota(jnp.int32, (C, C), 0)
    col = lax.broadcasted_iota(jnp.int32, (C, C), 1)
    strict = col < row
    incl = col <= row
    blk16 = (row // 16) == (col // 16)
    blk32 = (row // 32) == (col // 32)
    eye = (row == col).astype(F32)

    H = range(RW_HEADS)
    sls = [slice(h * N, (h + 1) * N) for h in H]
    A = [A_all[:, s] for s in sls]
    Bm = [B_all[:, s] for s in sls]
    Km = [K_all[:, s] for s in sls]
    R = [R_all[:, s] for s in sls]
    V = [v_all[:, s] for s in sls]
    S0 = [s_ref[h] for h in H]
    AR = [jnp.concatenate([A[h], R[h]], axis=0) for h in H]
    QB = [_dot_nt(AR[h], Bm[h]) for h in H]
    QK = [_dot_nt(AR[h], Km[h]) for h in H]
    Lb = [jnp.where(strict, QB[h][:C], 0.0) for h in H]
    Lk = [jnp.where(strict, QK[h][:C], 0.0) for h in H]
    Mb = [jnp.where(incl, QB[h][C:], 0.0) for h in H]
    Mk = [jnp.where(incl, QK[h][C:], 0.0) for h in H]
    Dg = [jnp.where(blk16, Lb[h], 0.0) for h in H]
    E1 = [jnp.where(blk32 & ~blk16, Lb[h], 0.0) for h in H]
    E2 = [jnp.where(blk32, 0.0, Lb[h]) for h in H]
    LkV = [_dot(Lk[h], V[h]) for h in H]
    D2 = [_dot(Dg[h], Dg[h]) for h in H]
    D4 = [_dot(D2[h], D2[h]) for h in H]
    D8 = [_dot(D4[h], D4[h]) for h in H]
    X = [eye + Dg[h] for h in H]
    X = [X[h] + _dot(X[h], D2[h]) for h in H]
    X = [X[h] + _dot(X[h], D4[h]) for h in H]
    X = [X[h] + _dot(X[h], D8[h]) for h in H]
    XE = [_dot(X[h], E1[h]) for h in H]
    X = [X[h] + _dot(XE[h], X[h]) for h in H]
    XE = [_dot(X[h], E2[h]) for h in H]
    Tm = [X[h] + _dot(XE[h], X[h]) for h in H]
    W1 = [_dot(Tm[h], A[h]) for h in H]
    W2 = [_dot(Tm[h], LkV[h]) for h in H]
    UR = [_dot_nt(jnp.concatenate([W1[h], R[h]], axis=0), S0[h]) for h in H]
    U = [UR[h][:C] + W2[h] for h in H]
    UV = [jnp.concatenate([U[h], V[h]], axis=0) for h in H]
    BK = [jnp.concatenate([Bm[h], Km[h]], axis=0) for h in H]
    for h in H:
        s_ref[h] = (S0[h] + _dot_tn(UV[h], BK[h])) * pc_all[:, sls[h]]
    O = [UR[h][C:] + _dot(Mb[h], U[h]) + _dot(Mk[h], V[h]) for h in H]
    for h in H:
        sl = sls[h]
        mu = jnp.mean(O[h], axis=-1, keepdims=True)
        var = jnp.mean(jnp.square(O[h] - mu), axis=-1, keepdims=True)
        y = (O[h] - mu) * lax.rsqrt(var + RW_GN_EPS) * lnw_ref[:, sl] + lnb_ref[:, sl]
        bonus = jnp.sum(r_all[:, sl] * k_all[:, sl] * rk_ref[:, sl], axis=-1, keepdims=True) * V[h]
        o_ref[:, sl] = y + bonus


def _rwscan(r, lw, k, v, kk, a, rk, lnw, lnb, tri, B, S):
    C = RW_CHUNK
    nc = S // C
    blk = pl.BlockSpec((C, RW_WIDTH), lambda b, c: (b * nc + c, 0))
    vec = pl.BlockSpec((1, RW_WIDTH), lambda b, c: (0, 0))
    return pl.pallas_call(
        _scan_kernel,
        grid=(B, nc),
        in_specs=[blk] * 6 + [vec, vec, vec, pl.BlockSpec((C, C), lambda b, c: (0, 0))],
        out_specs=blk,
        out_shape=jax.ShapeDtypeStruct((B * S, RW_WIDTH), F32),
        scratch_shapes=[pltpu.VMEM((RW_HEADS, RW_HEAD_DIM, RW_HEAD_DIM), F32)],
        compiler_params=_cparams("parallel", "arbitrary"),
        name="rwscan",
    )(r, lw, k, v, kk, a, rk, lnw, lnb, tri)


def _mlaprep_kernel(p_ref, pos_ref, qnw_ref, wuq_ref, kvnw_ref, wuk_ref, wuv_ref, seg_ref,
                    qlw_ref, klw_ref, pew_ref, freq_ref, q_o, k_o, v_o):
    p = p_ref[...]
    qn = _rms(p[:, :Q_LORA], qnw_ref[...]).astype(BF16)
    kvn = _rms(p[:, MLA_OFF_KV:MLA_OFF_PE], kvnw_ref[...]).astype(BF16)
    q_all = jnp.dot(qn, wuq_ref[...], preferred_element_type=F32)
    k_all = jnp.dot(kvn, wuk_ref[...], preferred_element_type=F32)
    v_all = jnp.dot(kvn, wuv_ref[...], preferred_element_type=F32)
    seg = seg_ref[...]
    ang = pos_ref[...].astype(F32) * freq_ref[...]
    cos = jnp.cos(ang)
    sin = jnp.sin(ang)
    lane = lax.broadcasted_iota(jnp.int32, ang.shape, 1)
    first_half = lane < QK_NOPE + QK_ROPE // 2

    def seg_norm(x, w):
        return x * lax.rsqrt(_dot_f32(x * x, seg) + NORM_EPS) * w

    def rope(x):
        partner = jnp.where(first_half, -pltpu.roll(x, LANES - QK_ROPE // 2, axis=1),
                            pltpu.roll(x, QK_ROPE // 2, axis=1))
        return x * cos + partner * sin

    kpe = rope(seg_norm(p[:, MLA_OFF_PE:MLA_PAD], pew_ref[...]))
    for h in range(MLA_HEADS):
        sl = slice(h * HEAD_SLAB, (h + 1) * HEAD_SLAB)
        q_o[0, h] = (rope(seg_norm(q_all[:, sl], qlw_ref[...])) * ATTN_SCALE).astype(BF16)
        k_o[0, h] = (seg_norm(k_all[:, sl], klw_ref[...]) + kpe).astype(BF16)
        v_o[0, h] = v_all[:, sl].astype(BF16)


def _mlaprep(p_mla, pos, qnw, wuq, kvnw, wuk, wuv, seg, qlw, klw, pew, freq, tm, B, S):
    ns = S // tm
    W = MLA_HEADS * HEAD_SLAB
    row = lambda b, s: (b * ns + s, 0)
    fixed = lambda b, s: (0, 0)
    vec = lambda n: pl.BlockSpec((1, n), fixed)
    oblk = pl.BlockSpec((1, MLA_HEADS, tm, HEAD_SLAB), lambda b, s: (b, 0, s, 0))
    out = jax.ShapeDtypeStruct((B, MLA_HEADS, S, HEAD_SLAB), BF16)
    return pl.pallas_call(
        _mlaprep_kernel,
        grid=(B, ns),
        in_specs=[pl.BlockSpec((tm, MLA_PAD), row),
                  pl.BlockSpec((tm, 1), row),
                  vec(Q_LORA), pl.BlockSpec((Q_LORA, W), fixed),
                  vec(KV_LORA), pl.BlockSpec((KV_LORA, W), fixed), pl.BlockSpec((KV_LORA, W), fixed),
                  pl.BlockSpec((LANES, LANES), fixed),
                  vec(LANES), vec(LANES), vec(LANES), vec(LANES)],
        out_specs=[oblk] * 3,
        out_shape=[out] * 3,
        compiler_params=_cparams("parallel", "parallel"),
        name="mlaprep",
    )(p_mla, pos, qnw, wuq, kvnw, wuk, wuv, seg, qlw, klw, pew, freq)


def _attn_kernel(q_ref, k_ref, v_ref, o_ref, m_ref, l_ref, acc_ref, *, blk):
    qi = pl.program_id(1)
    ki = pl.program_id(2)
    H = range(MLA_HEADS)

    @pl.when(ki == 0)
    def _():
        m_ref[...] = jnp.full_like(m_ref, NEG_INF)
        l_ref[...] = jnp.zeros_like(l_ref)
        acc_ref[...] = jnp.zeros_like(acc_ref)

    def update(masked):
        s = [lax.dot_general(q_ref[0, h], k_ref[0, h], (((1,), (1,)), ((), ())),
                             preferred_element_type=F32) for h in H]
        if masked:
            row = lax.broadcasted_iota(jnp.int32, (blk, blk), 0)
            col = lax.broadcasted_iota(jnp.int32, (blk, blk), 1)
            s = [jnp.where(col <= row, s[h], jnp.finfo(F32).min) for h in H]
        m_prev = [m_ref[h] for h in H]
        m_new = [jnp.maximum(m_prev[h], jnp.max(s[h], axis=-1, keepdims=True)) for h in H]
        alpha = [jnp.exp(m_prev[h] - m_new[h]) for h in H]
        p = [jnp.exp(s[h] - m_new[h]) for h in H]
        pv = [jnp.dot(p[h].astype(BF16), v_ref[0, h], preferred_element_type=F32) for h in H]
        for h in H:
            l_ref[h] = alpha[h] * l_ref[h] + jnp.sum(p[h], axis=-1, keepdims=True)
            acc_ref[h] = alpha[h] * acc_ref[h] + pv[h]
            m_ref[h] = m_new[h]

    @pl.when(ki < qi)
    def _():
        update(False)

    @pl.when(ki == qi)
    def _():
        update(True)
        for h in H:
            o_ref[0, :, h * HEAD_SLAB:(h + 1) * HEAD_SLAB] = (acc_ref[h] / l_ref[h]).astype(BF16)


def _attention(q, k, v, blk):
    B, H, S, W = q.shape
    n = S // blk
    qspec = pl.BlockSpec((1, H, blk, W), lambda b, i, j: (b, 0, i, 0))
    kspec = pl.BlockSpec((1, H, blk, W), lambda b, i, j: (b, 0, jnp.minimum(i, j), 0))
    return pl.pallas_call(
        functools.partial(_attn_kernel, blk=blk),
        grid=(B, n, n),
        in_specs=[qspec, kspec, kspec],
        out_specs=pl.BlockSpec((1, blk, H * W), lambda b, i, j: (b, i, 0)),
        out_shape=jax.ShapeDtypeStruct((B, S, H * W), BF16),
        scratch_shapes=[pltpu.VMEM((H, blk, 1), F32), pltpu.VMEM((H, blk, 1), F32),
                        pltpu.VMEM((H, blk, W), F32)],
        compiler_params=_cparams("parallel", "parallel", "arbitrary"),
        name="attention",
    )(q, k, v)


def _merge_kernel(x_ref, y_ref, grw_ref, o_ref, g_ref, wro_ref, wmo_ref, wout_ref, n2_ref, wq_ref,
                  h_o, hn_o, qp_o):
    ya = _dot(y_ref[...] * grw_ref[...], wro_ref[...])
    yb = jnp.dot(o_ref[...], wmo_ref[...], preferred_element_type=F32)
    g = g_ref[...].astype(F32)
    mixed = _sigmoid(g[:, :D_MODEL]) * ya + _sigmoid(g[:, D_MODEL:]) * yb
    h = x_ref[...] + _dot(mixed, wout_ref[...])
    hn = _rms(h, n2_ref[...]).astype(BF16)
    h_o[...] = h
    hn_o[...] = hn
    qp = jnp.dot(hn, wq_ref[...], preferred_element_type=F32).astype(BF16)
    for j in range(2 * PEER_HEADS):
        qp_o[j] = qp[:, j * LANES:(j + 1) * LANES]


def _merge(x2, y_rw, g_rw, o_mla, g, wro, wmo, wout, n2, wq, tm):
    T = x2.shape[0]
    row = lambda i: (i, 0)
    fixed = lambda i: (0, 0)
    W = MLA_HEADS * HEAD_SLAB
    NQ = 2 * PEER_HEADS
    return pl.pallas_call(
        _merge_kernel,
        grid=(T // tm,),
        in_specs=[pl.BlockSpec((tm, D_MODEL), row),
                  pl.BlockSpec((tm, RW_WIDTH), row),
                  pl.BlockSpec((tm, RW_WIDTH), row),
                  pl.BlockSpec((tm, W), row),
                  pl.BlockSpec((tm, G_PAD), row),
                  pl.BlockSpec((RW_WIDTH, D_MODEL), fixed),
                  pl.BlockSpec((W, D_MODEL), fixed),
                  pl.BlockSpec((D_MODEL, D_MODEL), fixed),
                  pl.BlockSpec((1, D_MODEL), fixed),
                  pl.BlockSpec((D_MODEL, NQ * LANES), fixed)],
        out_specs=[pl.BlockSpec((tm, D_MODEL), row),
                   pl.BlockSpec((tm, D_MODEL), row),
                   pl.BlockSpec((NQ, tm, LANES), lambda i: (0, i, 0))],
        out_shape=[jax.ShapeDtypeStruct((T, D_MODEL), F32),
                   jax.ShapeDtypeStruct((T, D_MODEL), BF16),
                   jax.ShapeDtypeStruct((NQ, T, LANES), BF16)],
        compiler_params=_cparams("parallel"),
        name="merge",
    )(x2, y_rw, g_rw, o_mla, g, wro, wmo, wout, n2, wq)


def _peer_kernel(qp_ref, keys_ref, hn_ref, h_ref, u_ref, vt_ref, out_ref,
                 s2_s, nb_s, e1_s, e2_s, ga_s, yt_s, *, TB, ET):
    e = pl.program_id(1)
    K = PEER_TOPK

    def top_rows(sc):
        rows = []
        for _ in range(K + 1):
            m = jnp.max(sc, axis=0, keepdims=True)
            rows.append(m)
            sc = jnp.where(sc >= m, NEG_INF, sc)
        return rows

    @pl.when(e == 0)
    def _prep():
        yt_s[...] = jnp.zeros_like(yt_s)
        ga_s[1] = jnp.zeros((ET, TB), BF16)

        def head(h, carry):
            s1 = _dot_nt(keys_ref[2 * h], qp_ref[2 * h])
            s2 = _dot_nt(keys_ref[2 * h + 1], qp_ref[2 * h + 1])
            a1 = top_rows(s1)
            a2 = top_rows(s2)
            a2_lo = jnp.concatenate(a2[:8], axis=0)
            pieces = [a1[0] + jnp.concatenate(a2[:K], axis=0)]
            pieces += [a1[i] + a2_lo for i in range(1, 8)]
            pieces += [jnp.concatenate(a1[8:K], axis=0) + a2[0]]
            pieces += [a1[0] + a2[K], a1[K] + a2[0], jnp.full((6, TB), NEG_INF, F32)]
            cand = jnp.concatenate(pieces, axis=0)
            top = a1[0] + a2[0]
            z = jnp.zeros_like(top)
            m = top
            for _ in range(K):
                m = jnp.max(cand, axis=0, keepdims=True)
                z = z + jnp.exp(m - top)
                cand = jnp.where(cand >= m, NEG_INF, cand)
            tau = 0.5 * (m + jnp.max(cand, axis=0, keepdims=True))
            s2_s[h] = s2
            nb_s[h] = tau - s1
            e1_s[h] = jnp.exp(s1 - a1[0]) / z
            e2_s[h] = jnp.exp(s2 - a2[0])
            return carry

        lax.fori_loop(0, PEER_HEADS, head, 0)

    n_tiles = pl.num_programs(1) - 1
    cur = e % 2
    n_i = ET // N_KEYS
    i0 = pl.multiple_of(jnp.minimum(e, n_tiles - 1) * n_i, n_i)
    tcs = [slice(tc * LANES, (tc + 1) * LANES) for tc in range(TB // LANES)]
    nb_rows = [[nb_s[h, pl.ds(i0, n_i), ts] for h in range(PEER_HEADS)] for ts in tcs]
    e1_rows = [[e1_s[h, pl.ds(i0, n_i), ts] for h in range(PEER_HEADS)] for ts in tcs]
    TH = TB // 2
    EH = ET // 2
    acts = {}

    def act_piece(p, mh):
        acts[p, mh] = lax.dot_general(
            u_ref[mh * EH:(mh + 1) * EH, :], hn_ref[p * TH:(p + 1) * TH, :],
            (((1,), (1,)), ((), ())), preferred_element_type=F32)

    def y_piece(p, dh):
        rs = slice(dh * (D_MODEL // 2), (dh + 1) * (D_MODEL // 2))
        cs = slice(p * TH, (p + 1) * TH)
        yt_s[rs, cs] += jnp.dot(vt_ref[rs, :], ga_s[1 - cur, :, cs], preferred_element_type=F32)

    def gate_unit(s, tc):
        ts = tcs[tc]
        G = [jnp.zeros((N_KEYS, LANES), F32) for _ in range(2)]
        for h in range(PEER_HEADS):
            s2t = s2_s[h, :, ts]
            e2t = e2_s[h, :, ts]
            for d in range(2):
                ii = 2 * s + d
                sel = jnp.where(s2t >= nb_rows[tc][h][ii:ii + 1], e2t, 0.0)
                G[d] = G[d] + sel * e1_rows[tc][h][ii:ii + 1]
        p, mh = tc // (TH // LANES), (2 * s * N_KEYS) // EH
        act = acts[p, mh]
        for d in range(2):
            r0 = (2 * s + d) * N_KEYS
            a = act[r0 - mh * EH:r0 - mh * EH + N_KEYS, ts.start - p * TH:ts.stop - p * TH]
            gelu = 0.5 * a * (1.0 + lax.erf(a * (1.0 / math.sqrt(2.0))))
            ga_s[cur, r0:r0 + N_KEYS, ts] = (G[d] * gelu).astype(BF16)

    mxu = [lambda: act_piece(0, 0), lambda: act_piece(0, 1), lambda: act_piece(1, 0),
           lambda: act_piece(1, 1), lambda: y_piece(0, 0), lambda: y_piece(0, 1),
           lambda: y_piece(1, 0), lambda: y_piece(1, 1)]
    units = [(s, tc) for tc2 in range(2) for s in range(n_i // 2) for tc in (2 * tc2, 2 * tc2 + 1)]
    mxu[0]()
    for j in range(8):
        if j + 1 < 8:
            mxu[j + 1]()
        for s, tc in units[2 * j:2 * j + 2]:
            gate_unit(s, tc)

    @pl.when(e == n_tiles)
    def _():
        out_ref[...] = h_ref[...] + yt_s[...].T


def _peer(qp, keys, hn, h, u, vt, TB, ET):
    T = hn.shape[0]
    NQ = 2 * PEER_HEADS
    tok = lambda t, e: (t, 0)
    n_tiles = N_EXPERTS // ET
    return pl.pallas_call(
        functools.partial(_peer_kernel, TB=TB, ET=ET),
        grid=(T // TB, n_tiles + 1),
        in_specs=[pl.BlockSpec((NQ, TB, LANES), lambda t, e: (0, t, 0)),
                  pl.BlockSpec((NQ, N_KEYS, LANES), lambda t, e: (0, 0, 0)),
                  pl.BlockSpec((TB, D_MODEL), tok),
                  pl.BlockSpec((TB, D_MODEL), tok),
                  pl.BlockSpec((ET, D_MODEL), lambda t, e: (jnp.minimum(e, n_tiles - 1), 0)),
                  pl.BlockSpec((D_MODEL, ET), lambda t, e: (0, jnp.maximum(e - 1, 0)))],
        out_specs=pl.BlockSpec((TB, D_MODEL), tok),
        out_shape=jax.ShapeDtypeStruct((T, D_MODEL), F32),
        scratch_shapes=[pltpu.VMEM((PEER_HEADS, N_KEYS, TB), F32)] * 4
        + [pltpu.VMEM((2, ET, TB), BF16), pltpu.VMEM((D_MODEL, TB), F32)],
        compiler_params=_cparams("parallel", "arbitrary"),
        name="peer",
    )(qp, keys, hn, h, u, vt)


def _pad_cols(w, n):
    return jnp.pad(w, ((0, 0), (0, n - w.shape[1])))


def _pad_rows(w, n):
    return jnp.pad(w, ((0, n - w.shape[0]), (0, 0)))


def _head_slabs(w, width):
    k = w.shape[0]
    w = w.reshape(k, MLA_HEADS, width)
    return jnp.pad(w, ((0, 0), (0, 0), (0, HEAD_SLAB - width))).reshape(k, MLA_HEADS * HEAD_SLAB)


def _lane_vec(nope, pe):
    v = jnp.zeros((LANES,), F32)
    if nope is not None:
        v = v.at[:QK_NOPE].set(nope)
    if pe is not None:
        v = v.at[QK_NOPE:QK_NOPE + QK_ROPE].set(pe)
    return v.reshape(1, LANES)


def kernel(x, positions, norm1_w, w_in, rw_mu, rw_w0, rw_w2, rw_a0, rw_a2, rw_g2, rw_k_k, rw_k_a,
           rw_r_k, rw_ln_w, rw_ln_b, rw_w_o, mla_q_norm_w, mla_w_uq, mla_kv_norm_w, mla_w_ukv,
           mla_qn_nope_w, mla_qn_pe_w, mla_kn_nope_w, mla_kn_pe_w, mla_w_o, w_out, norm2_w,
           peer_w_query, peer_sub_keys, peer_u, peer_v):
    B, S, D = x.shape
    T = B * S
    x2 = x.reshape(T, D)
    row = lambda v: v.reshape(1, -1).astype(F32)

    o_q = RW_SHIFT_WIDTH
    o_kv = o_q + Q_LORA
    o_pe = o_kv + KV_LORA
    o_g = o_pe + QK_ROPE
    seg3 = 3 * RW_WIDTH
    w_rw = jnp.concatenate([
        w_in[:, :seg3],
        _pad_cols(w_in[:, seg3:seg3 + DECAY_LORA], LANES),
        _pad_cols(w_in[:, seg3 + DECAY_LORA:seg3 + DECAY_LORA + AAA_LORA], LANES),
        _pad_cols(w_in[:, seg3 + DECAY_LORA + AAA_LORA:o_q], GATE_PAD)], axis=1)
    pe_slab = jnp.zeros((D, LANES), F32).at[:, QK_NOPE:QK_NOPE + QK_ROPE].set(w_in[:, o_pe:o_g])
    w_cat = jnp.concatenate([w_rw, w_in[:, o_q:o_pe], pe_slab, w_in[:, o_g:]], axis=1).astype(BF16)
    mu = jnp.concatenate([
        rw_mu[:seg3],
        jnp.pad(rw_mu[seg3:seg3 + DECAY_LORA], (0, LANES - DECAY_LORA)),
        jnp.pad(rw_mu[seg3 + DECAY_LORA:seg3 + DECAY_LORA + AAA_LORA], (0, LANES - AAA_LORA)),
        jnp.pad(rw_mu[seg3 + DECAY_LORA + AAA_LORA:], (0, GATE_PAD - GATE_LORA))]).reshape(1, RW_PAD)

    tm = min(256, S)
    p_rw, p_mla, g = _inproj(x2, row(norm1_w), w_cat, tm)

    hid = np.arange(RW_WIDTH) // RW_HEAD_DIM
    ones_blk = jnp.asarray((hid[:, None] == hid[None, :]).astype(np.float32))
    tp = min(512, S)
    r, lw, k2, v, kk, a, g_rw = _rwprep(
        p_rw, mu, row(rw_w0), _pad_rows(rw_w2, LANES), row(rw_a0), _pad_rows(rw_a2, LANES),
        _pad_rows(rw_g2, GATE_PAD), row(rw_k_k), row(rw_k_a), ones_blk, tp, S)
    tri = jnp.asarray(np.tril(np.ones((RW_CHUNK, RW_CHUNK), np.float32)))
    y_rw = _rwscan(r, lw, k2, v, kk, a, row(rw_r_k), row(rw_ln_w), row(rw_ln_b), tri, B, S)

    dq = QK_NOPE + QK_ROPE
    wuq = _head_slabs(mla_w_uq, dq).astype(BF16)
    ukv = mla_w_ukv.reshape(KV_LORA, MLA_HEADS, QK_NOPE + V_DIM)
    wuk = _head_slabs(ukv[:, :, :QK_NOPE].reshape(KV_LORA, -1), QK_NOPE).astype(BF16)
    wuv = _head_slabs(ukv[:, :, QK_NOPE:].reshape(KV_LORA, -1), V_DIM).astype(BF16)
    lane = np.arange(LANES)
    in_nope = lane < QK_NOPE
    in_pe = (lane >= QK_NOPE) & (lane < dq)
    seg = (in_nope[:, None] & in_nope[None, :]) / QK_NOPE + (in_pe[:, None] & in_pe[None, :]) / QK_ROPE
    half = QK_ROPE // 2
    inv_freq = ROPE_BASE ** (-jnp.arange(half, dtype=F32) / half)
    freq = jnp.zeros((LANES,), F32).at[QK_NOPE:dq].set(jnp.concatenate([inv_freq, inv_freq]))
    q, k, vv = _mlaprep(
        p_mla, positions.reshape(T, 1).astype(jnp.int32), row(mla_q_norm_w), wuq,
        row(mla_kv_norm_w), wuk, wuv, jnp.asarray(seg.astype(np.float32)),
        _lane_vec(mla_qn_nope_w, mla_qn_pe_w), _lane_vec(mla_kn_nope_w, None),
        _lane_vec(None, mla_kn_pe_w), freq.reshape(1, LANES), tp, B, S)
    o_mla = _attention(q, k, vv, min(512, S)).reshape(T, MLA_HEADS * HEAD_SLAB)

    wmo = jnp.pad(mla_w_o.reshape(MLA_HEADS, V_DIM, D), ((0, 0), (0, HEAD_SLAB - V_DIM), (0, 0)))
    wmo = wmo.reshape(MLA_HEADS * HEAD_SLAB, D).astype(BF16)
    h, hn, qp = _merge(x2, y_rw, g_rw, o_mla, g, rw_w_o.astype(BF16), wmo, w_out.astype(BF16),
                       row(norm2_w), peer_w_query.astype(BF16), tm)

    keys = peer_sub_keys.reshape(2 * PEER_HEADS, N_KEYS, D_QUERY // 2).astype(BF16)
    out = _peer(qp, keys, hn, h, peer_u.astype(BF16), peer_v.T.astype(BF16), min(512, T), 1024)
    return out.reshape(B, S, D)
```

```python
import functools
import math

import numpy as np
import jax
import jax.numpy as jnp
from jax import lax
from jax.experimental import pallas as pl
from jax.experimental.pallas import tpu as pltpu

F32 = jnp.float32
BF16 = jnp.bfloat16

D_MODEL = 1024
NORM_EPS = 1e-6
RW_HEADS = 8
RW_HEAD_DIM = 64
RW_WIDTH = RW_HEADS * RW_HEAD_DIM
DECAY_LORA = 64
AAA_LORA = 64
GATE_LORA = 160
RW_GN_EPS = 64e-5
RW_SHIFT_WIDTH = 3 * RW_WIDTH + DECAY_LORA + AAA_LORA + GATE_LORA
MLA_HEADS = 8
QK_NOPE = 64
QK_ROPE = 32
V_DIM = 64
Q_LORA = 384
KV_LORA = 256
ROPE_BASE = 10000.0
ATTN_SCALE = 1.0 / math.sqrt(QK_NOPE + QK_ROPE)
PEER_HEADS = 8
N_KEYS = 128
N_EXPERTS = N_KEYS * N_KEYS
PEER_TOPK = 16
D_QUERY = 256

LANES = 128
RW_OFF_WD = 3 * RW_WIDTH
RW_OFF_AD = RW_OFF_WD + LANES
RW_OFF_GD = RW_OFF_AD + LANES
GATE_PAD = 2 * LANES
RW_PAD = RW_OFF_GD + GATE_PAD
MLA_OFF_KV = Q_LORA
MLA_OFF_PE = Q_LORA + KV_LORA
MLA_PAD = MLA_OFF_PE + LANES
G_PAD = 2 * D_MODEL
IN_PAD = RW_PAD + MLA_PAD + G_PAD
HEAD_SLAB = LANES

RW_CHUNK = 64
VMEM_LIMIT = 56 * 1024 * 1024

NEG_INF = float("-inf")


def _cparams(*sem):
    return pltpu.CompilerParams(dimension_semantics=sem, vmem_limit_bytes=VMEM_LIMIT)


def _dot(a, b):
    return jnp.dot(a.astype(BF16), b.astype(BF16), preferred_element_type=F32)


def _dot_nt(a, b):
    return lax.dot_general(a.astype(BF16), b.astype(BF16), (((1,), (1,)), ((), ())),
                           preferred_element_type=F32)


def _dot_tn(a, b):
    return lax.dot_general(a.astype(BF16), b.astype(BF16), (((0,), (0,)), ((), ())),
                           preferred_element_type=F32)


def _split(a):
    hi = a.astype(BF16)
    return hi, (a - hi.astype(F32)).astype(BF16)


def _dot_split_lhs(a, b):
    hi, lo = _split(a)
    return (jnp.dot(hi, b, preferred_element_type=F32) + jnp.dot(lo, b, preferred_element_type=F32))


def _dot_split(a, b):
    ah, al = _split(a)
    bh, bl = _split(b)
    return (jnp.dot(ah, bh, preferred_element_type=F32) + jnp.dot(ah, bl, preferred_element_type=F32)
            + jnp.dot(al, bh, preferred_element_type=F32))


def _sigmoid(x):
    return 1.0 / (1.0 + jnp.exp(-x))


def _rms(x, w):
    return x * lax.rsqrt(jnp.mean(x * x, axis=-1, keepdims=True) + NORM_EPS) * w


def _inproj_kernel(x_ref, nw_ref, w_ref, orw_ref, omla_ref, og_ref):
    xb = _rms(x_ref[...], nw_ref[...]).astype(BF16)
    orw_ref[...] = jnp.dot(xb, w_ref[:, :RW_PAD], preferred_element_type=F32)
    omla_ref[...] = jnp.dot(xb, w_ref[:, RW_PAD:RW_PAD + MLA_PAD], preferred_element_type=F32)
    og_ref[...] = jnp.dot(xb, w_ref[:, RW_PAD + MLA_PAD:], preferred_element_type=F32).astype(BF16)


def _inproj(x2, norm1_w, w_cat, tm):
    T = x2.shape[0]
    row = lambda i: (i, 0)
    fixed = lambda i: (0, 0)
    return pl.pallas_call(
        _inproj_kernel,
        grid=(T // tm,),
        in_specs=[pl.BlockSpec((tm, D_MODEL), row),
                  pl.BlockSpec((1, D_MODEL), fixed),
                  pl.BlockSpec((D_MODEL, IN_PAD), fixed)],
        out_specs=[pl.BlockSpec((tm, RW_PAD), row),
                   pl.BlockSpec((tm, MLA_PAD), row),
                   pl.BlockSpec((tm, G_PAD), row)],
        out_shape=[jax.ShapeDtypeStruct((T, RW_PAD), F32),
                   jax.ShapeDtypeStruct((T, MLA_PAD), F32),
                   jax.ShapeDtypeStruct((T, G_PAD), BF16)],
        compiler_params=_cparams("parallel"),
        name="inproj",
    )(x2, norm1_w, w_cat)


def _rwprep_kernel(p_ref, pp_ref, mu_ref, w0_ref, w2_ref, a0_ref, a2_ref, g2_ref, kkw_ref,
                   kaw_ref, ones_ref, r_o, lw_o, k_o, v_o, kk_o, a_o, g_o, *, tiles_per_seq):
    i = pl.program_id(0)
    p = p_ref[...]
    prev_row = jnp.where(i % tiles_per_seq == 0, 0.0, pp_ref[7:8, :])
    rolled = pltpu.roll(p, 1, axis=0)
    row = lax.broadcasted_iota(jnp.int32, p.shape, 0)
    p_prev = jnp.where(row == 0, prev_row, rolled)
    pm = p + mu_ref[...] * (p_prev - p)
    r = pm[:, 0:RW_WIDTH]
    k = pm[:, RW_WIDTH:2 * RW_WIDTH]
    v = pm[:, 2 * RW_WIDTH:3 * RW_WIDTH]
    wd = pm[:, RW_OFF_WD:RW_OFF_AD]
    ad = pm[:, RW_OFF_AD:RW_OFF_GD]
    gd = pm[:, RW_OFF_GD:RW_PAD]
    z = -(w0_ref[...] + _dot_split(jnp.tanh(wd), w2_ref[...]))
    softplus = jnp.maximum(z, 0.0) + jnp.log1p(jnp.exp(-jnp.abs(z)))
    lw_o[...] = -jnp.exp(-softplus - 0.5)
    a = _sigmoid(a0_ref[...] + _dot_split(ad, a2_ref[...]))
    g_o[...] = _dot_split(_sigmoid(gd), g2_ref[...])
    kk0 = k * kkw_ref[...]
    ss = _dot_split_lhs(kk0 * kk0, ones_ref[...])
    kk_o[...] = kk0 / jnp.maximum(jnp.sqrt(ss), 1e-12)
    k_o[...] = k * (1.0 + (a - 1.0) * kaw_ref[...])
    r_o[...] = r
    v_o[...] = v
    a_o[...] = a


def _rwprep(p_rw, mu, w0, w2, a0, a2, g2, kkw, kaw, ones_blk, tm, S):
    T = p_rw.shape[0]
    row = lambda i: (i, 0)
    fixed = lambda i: (0, 0)
    sub = tm // 8
    vec = lambda n: pl.BlockSpec((1, n), fixed)
    out = jax.ShapeDtypeStruct((T, RW_WIDTH), F32)
    return pl.pallas_call(
        functools.partial(_rwprep_kernel, tiles_per_seq=S // tm),
        grid=(T // tm,),
        in_specs=[pl.BlockSpec((tm, RW_PAD), row),
                  pl.BlockSpec((8, RW_PAD), lambda i: (jnp.maximum(i * sub - 1, 0), 0)),
                  vec(RW_PAD), vec(RW_WIDTH),
                  pl.BlockSpec((LANES, RW_WIDTH), fixed),
                  vec(RW_WIDTH),
                  pl.BlockSpec((LANES, RW_WIDTH), fixed),
                  pl.BlockSpec((GATE_PAD, RW_WIDTH), fixed),
                  vec(RW_WIDTH), vec(RW_WIDTH),
                  pl.BlockSpec((RW_WIDTH, RW_WIDTH), fixed)],
        out_specs=[pl.BlockSpec((tm, RW_WIDTH), row)] * 7,
        out_shape=[out] * 7,
        compiler_params=_cparams("parallel"),
        name="rwprep",
    )(p_rw, p_rw, mu, w0, w2, a0, a2, g2, kkw, kaw, ones_blk)


def _scan_kernel(r_ref, lw_ref, k_ref, v_ref, kk_ref, a_ref, rk_ref, lnw_ref, lnb_ref, tri_ref,
                 o_ref, s_ref):
    c = pl.program_id(1)

    @pl.when(c == 0)
    def _():
        s_ref[...] = jnp.zeros_like(s_ref)

    C = r_ref.shape[0]
    N = RW_HEAD_DIM
    lw = lw_ref[...]
    lw_hi, lw_lo = _split(lw)
    tri = tri_ref[...]
    cum = (jnp.dot(tri, lw_hi, preferred_element_type=F32)
           + jnp.dot(tri, lw_lo, preferred_element_type=F32))
    e_pos = jnp.exp(cum)
    e_prev = jnp.exp(cum - lw)
    e_neg = jnp.exp(-cum)
    kk = kk_ref[...]
    r_all = r_ref[...]
    k_all = k_ref[...]
    v_all = v_ref[...]
    A_all = -kk * e_prev
    B_all = kk * a_ref[...] * e_neg
    K_all = k_all * e_neg
    R_all = r_all * e_pos
    pc_all = e_pos[C - 1:C, :]

    row = lax.broadcasted_iota(jnp.int32, (C, C), 0)
    col = lax.broadcasted_iota(jnp.int32, (C, C), 1)
    strict = col < row
    incl = col <= row
    blk16 = (row // 16) == (col // 16)
    blk32 = (row // 32) == (col // 32)
    eye = (row == col).astype(F32)

    H = range(RW_HEADS)
    sls = [slice(h * N, (h + 1) * N) for h in H]
    A = [A_all[:, s] for s in sls]
    Bm = [B_all[:, s] for s in sls]
    Km = [K_all[:, s] for s in sls]
    R = [R_all[:, s] for s in sls]
    V = [v_all[:, s] for s in sls]
    S0 = [s_ref[h] for h in H]
    AR = [jnp.concatenate([A[h], R[h]], axis=0) for h in H]
    QB = [_dot_nt(AR[h], Bm[h]) for h in H]
    QK = [_dot_nt(AR[h], Km[h]) for h in H]
    Lb = [jnp.where(strict, QB[h][:C], 0.0) for h in H]
    Lk = [jnp.where(strict, QK[h][:C], 0.0) for h in H]
    Mb = [jnp.where(incl, QB[h][C:], 0.0) for h in H]
    Mk = [jnp.where(incl, QK[h][C:], 0.0) for h in H]
    Dg = [jnp.where(blk16, Lb[h], 0.0) for h in H]
    E1 = [jnp.where(blk32 & ~blk16, Lb[h], 0.0) for h in H]
    E2 = [jnp.where(blk32, 0.0, Lb[h]) for h in H]
    LkV = [_dot(Lk[h], V[h]) for h in H]
    D2 = [_dot(Dg[h], Dg[h]) for h in H]
    D4 = [_dot(D2[h], D2[h]) for h in H]
    D8 = [_dot(D4[h], D4[h]) for h in H]
    X = [eye + Dg[h] for h in H]
    X = [X[h] + _dot(X[h], D2[h]) for h in H]
    X = [X[h] + _dot(X[h], D4[h]) for h in H]
    X = [X[h] + _dot(X[h], D8[h]) for h in H]
    XE = [_dot(X[h], E1[h]) for h in H]
    X = [X[h] + _dot(XE[h], X[h]) for h in H]
    XE = [_dot(X[h], E2[h]) for h in H]
    Tm = [X[h] + _dot(XE[h], X[h]) for h in H]
    W1 = [_dot(Tm[h], A[h]) for h in H]
    W2 = [_dot(Tm[h], LkV[h]) for h in H]
    UR = [_dot_nt(jnp.concatenate([W1[h], R[h]], axis=0), S0[h]) for h in H]
    U = [UR[h][:C] + W2[h] for h in H]
    UV = [jnp.concatenate([U[h], V[h]], axis=0) for h in H]
    BK = [jnp.concatenate([Bm[h], Km[h]], axis=0) for h in H]
    for h in H:
        s_ref[h] = (S0[h] + _dot_tn(UV[h], BK[h])) * pc_all[:, sls[h]]
    O = [UR[h][C:] + _dot(Mb[h], U[h]) + _dot(Mk[h], V[h]) for h in H]
    for h in H:
        sl = sls[h]
        mu = jnp.mean(O[h], axis=-1, keepdims=True)
        var = jnp.mean(jnp.square(O[h] - mu), axis=-1, keepdims=True)
        y = (O[h] - mu) * lax.rsqrt(var + RW_GN_EPS) * lnw_ref[:, sl] + lnb_ref[:, sl]
        bonus = jnp.sum(r_all[:, sl] * k_all[:, sl] * rk_ref[:, sl], axis=-1, keepdims=True) * V[h]
        o_ref[:, sl] = y + bonus


def _rwscan(r, lw, k, v, kk, a, rk, lnw, lnb, tri, B, S):
    C = RW_CHUNK
    nc = S // C
    blk = pl.BlockSpec((C, RW_WIDTH), lambda b, c: (b * nc + c, 0))
    vec = pl.BlockSpec((1, RW_WIDTH), lambda b, c: (0, 0))
    return pl.pallas_call(
        _scan_kernel,
        grid=(B, nc),
        in_specs=[blk] * 6 + [vec, vec, vec, pl.BlockSpec((C, C), lambda b, c: (0, 0))],
        out_specs=blk,
        out_shape=jax.ShapeDtypeStruct((B * S, RW_WIDTH), F32),
        scratch_shapes=[pltpu.VMEM((RW_HEADS, RW_HEAD_DIM, RW_HEAD_DIM), F32)],
        compiler_params=_cparams("parallel", "arbitrary"),
        name="rwscan",
    )(r, lw, k, v, kk, a, rk, lnw, lnb, tri)


def _mlaprep_kernel(p_ref, pos_ref, qnw_ref, wuq_ref, kvnw_ref, wuk_ref, wuv_ref, seg_ref,
                    qlw_ref, klw_ref, pew_ref, freq_ref, q_o, k_o, v_o):
    p = p_ref[...]
    qn = _rms(p[:, :Q_LORA], qnw_ref[...]).astype(BF16)
    kvn = _rms(p[:, MLA_OFF_KV:MLA_OFF_PE], kvnw_ref[...]).astype(BF16)
    q_all = jnp.dot(qn, wuq_ref[...], preferred_element_type=F32)
    k_all = jnp.dot(kvn, wuk_ref[...], preferred_element_type=F32)
    v_all = jnp.dot(kvn, wuv_ref[...], preferred_element_type=F32)
    seg = seg_ref[...]
    ang = pos_ref[...].astype(F32) * freq_ref[...]
    cos = jnp.cos(ang)
    sin = jnp.sin(ang)
    lane = lax.broadcasted_iota(jnp.int32, ang.shape, 1)
    first_half = lane < QK_NOPE + QK_ROPE // 2
    one_lane = (lane == V_DIM).astype(F32)

    def seg_norm(x, w):
        return x * lax.rsqrt(_dot_split_lhs(x * x, seg) + NORM_EPS) * w

    def rope(x):
        partner = jnp.where(first_half, -pltpu.roll(x, LANES - QK_ROPE // 2, axis=1),
                            pltpu.roll(x, QK_ROPE // 2, axis=1))
        return x * cos + partner * sin

    kpe = rope(seg_norm(p[:, MLA_OFF_PE:MLA_PAD], pew_ref[...]))
    for h in range(MLA_HEADS):
        sl = slice(h * HEAD_SLAB, (h + 1) * HEAD_SLAB)
        q_o[0, h] = (rope(seg_norm(q_all[:, sl], qlw_ref[...])) * ATTN_SCALE).astype(BF16)
        k_o[0, h] = (seg_norm(k_all[:, sl], klw_ref[...]) + kpe).astype(BF16)
        v_o[0, h] = (v_all[:, sl] + one_lane).astype(BF16)


def _mlaprep(p_mla, pos, qnw, wuq, kvnw, wuk, wuv, seg, qlw, klw, pew, freq, tm, B, S):
    ns = S // tm
    W = MLA_HEADS * HEAD_SLAB
    row = lambda b, s: (b * ns + s, 0)
    fixed = lambda b, s: (0, 0)
    vec = lambda n: pl.BlockSpec((1, n), fixed)
    oblk = pl.BlockSpec((1, MLA_HEADS, tm, HEAD_SLAB), lambda b, s: (b, 0, s, 0))
    out = jax.ShapeDtypeStruct((B, MLA_HEADS, S, HEAD_SLAB), BF16)
    return pl.pallas_call(
        _mlaprep_kernel,
        grid=(B, ns),
        in_specs=[pl.BlockSpec((tm, MLA_PAD), row),
                  pl.BlockSpec((tm, 1), row),
                  vec(Q_LORA), pl.BlockSpec((Q_LORA, W), fixed),
                  vec(KV_LORA), pl.BlockSpec((KV_LORA, W), fixed), pl.BlockSpec((KV_LORA, W), fixed),
                  pl.BlockSpec((LANES, LANES), fixed),
                  vec(LANES), vec(LANES), vec(LANES), vec(LANES)],
        out_specs=[oblk] * 3,
        out_shape=[out] * 3,
        compiler_params=_cparams("parallel", "parallel"),
        name="mlaprep",
    )(p_mla, pos, qnw, wuq, kvnw, wuk, wuv, seg, qlw, klw, pew, freq)


def _attn_kernel(q_ref, k_ref, v_ref, o_ref, m_ref, acc_ref, *, blk):
    qi = pl.program_id(1)
    ki = pl.program_id(2)
    H = range(MLA_HEADS)

    @pl.when(ki == 0)
    def _():
        m_ref[...] = jnp.full_like(m_ref, NEG_INF)
        acc_ref[...] = jnp.zeros_like(acc_ref)

    def update(masked):
        s = [lax.dot_general(q_ref[0, h], k_ref[0, h], (((1,), (1,)), ((), ())),
                             preferred_element_type=F32) for h in H]
        if masked:
            row = lax.broadcasted_iota(jnp.int32, (blk, blk), 0)
            col = lax.broadcasted_iota(jnp.int32, (blk, blk), 1)
            s = [jnp.where(col <= row, s[h], jnp.finfo(F32).min) for h in H]
        m_prev = [m_ref[h] for h in H]
        m_new = [jnp.maximum(m_prev[h], jnp.max(s[h], axis=-1, keepdims=True)) for h in H]
        alpha = [jnp.exp(m_prev[h] - m_new[h]) for h in H]
        p = [jnp.exp(s[h] - jnp.tile(m_new[h], (1, blk // LANES))) for h in H]
        pv = [jnp.dot(p[h].astype(BF16), v_ref[0, h], preferred_element_type=F32) for h in H]
        for h in H:
            acc_ref[h] = alpha[h] * acc_ref[h] + pv[h]
            m_ref[h] = m_new[h]

    @pl.when(ki < qi)
    def _():
        update(False)

    @pl.when(ki == qi)
    def _():
        update(True)
        for h in H:
            acc = acc_ref[h]
            denom = jnp.broadcast_to(acc[:, V_DIM:V_DIM + 1], acc.shape)
            o_ref[0, :, h * HEAD_SLAB:(h + 1) * HEAD_SLAB] = (acc / denom).astype(BF16)


def _attention(q, k, v, blk):
    B, H, S, W = q.shape
    n = S // blk
    qspec = pl.BlockSpec((1, H, blk, W), lambda b, i, j: (b, 0, i, 0))
    kspec = pl.BlockSpec((1, H, blk, W), lambda b, i, j: (b, 0, jnp.minimum(i, j), 0))
    return pl.pallas_call(
        functools.partial(_attn_kernel, blk=blk),
        grid=(B, n, n),
        in_specs=[qspec, kspec, kspec],
        out_specs=pl.BlockSpec((1, blk, H * W), lambda b, i, j: (b, i, 0)),
        out_shape=jax.ShapeDtypeStruct((B, S, H * W), BF16),
        scratch_shapes=[pltpu.VMEM((H, blk, LANES), F32), pltpu.VMEM((H, blk, W), F32)],
        compiler_params=_cparams("parallel", "parallel", "arbitrary"),
        name="attention",
    )(q, k, v)


def _merge_kernel(x_ref, y_ref, grw_ref, o_ref, g_ref, wro_ref, wmo_ref, wout_ref, n2_ref, wq_ref,
                  h_o, hn_o, qp_o):
    ya = _dot(y_ref[...] * grw_ref[...], wro_ref[...])
    yb = jnp.dot(o_ref[...], wmo_ref[...], preferred_element_type=F32)
    g = g_ref[...].astype(F32)
    mixed = _sigmoid(g[:, :D_MODEL]) * ya + _sigmoid(g[:, D_MODEL:]) * yb
    h = x_ref[...] + _dot(mixed, wout_ref[...])
    hn = _rms(h, n2_ref[...]).astype(BF16)
    h_o[...] = h
    hn_o[...] = hn
    qp = jnp.dot(hn, wq_ref[...], preferred_element_type=F32).astype(BF16)
    for j in range(2 * PEER_HEADS):
        qp_o[j] = qp[:, j * LANES:(j + 1) * LANES]


def _merge(x2, y_rw, g_rw, o_mla, g, wro, wmo, wout, n2, wq, tm):
    T = x2.shape[0]
    row = lambda i: (i, 0)
    fixed = lambda i: (0, 0)
    W = MLA_HEADS * HEAD_SLAB
    NQ = 2 * PEER_HEADS
    return pl.pallas_call(
        _merge_kernel,
        grid=(T // tm,),
        in_specs=[pl.BlockSpec((tm, D_MODEL), row),
                  pl.BlockSpec((tm, RW_WIDTH), row),
                  pl.BlockSpec((tm, RW_WIDTH), row),
                  pl.BlockSpec((tm, W), row),
                  pl.BlockSpec((tm, G_PAD), row),
                  pl.BlockSpec((RW_WIDTH, D_MODEL), fixed),
                  pl.BlockSpec((W, D_MODEL), fixed),
                  pl.BlockSpec((D_MODEL, D_MODEL), fixed),
                  pl.BlockSpec((1, D_MODEL), fixed),
                  pl.BlockSpec((D_MODEL, NQ * LANES), fixed)],
        out_specs=[pl.BlockSpec((tm, D_MODEL), row),
                   pl.BlockSpec((tm, D_MODEL), row),
                   pl.BlockSpec((NQ, tm, LANES), lambda i: (0, i, 0))],
        out_shape=[jax.ShapeDtypeStruct((T, D_MODEL), F32),
                   jax.ShapeDtypeStruct((T, D_MODEL), BF16),
                   jax.ShapeDtypeStruct((NQ, T, LANES), BF16)],
        compiler_params=_cparams("parallel"),
        name="merge",
    )(x2, y_rw, g_rw, o_mla, g, wro, wmo, wout, n2, wq)


def _peer_kernel(qp_ref, keys_ref, hn_ref, h_ref, u_ref, vt_ref, out_ref,
                 s2_s, nb_s, e1_s, e2_s, act_s, ga_s, yt_s, *, TB, ET):
    e = pl.program_id(1)
    K = PEER_TOPK

    def top_rows(sc):
        rows = []
        for _ in range(K + 1):
            m = jnp.max(sc, axis=0, keepdims=True)
            rows.append(m)
            sc = jnp.where(sc >= m, NEG_INF, sc)
        return rows

    @pl.when(e == 0)
    def _prep():
        yt_s[...] = jnp.zeros_like(yt_s)

        def head(h, carry):
            s1 = _dot_nt(keys_ref[2 * h], qp_ref[2 * h])
            s2 = _dot_nt(keys_ref[2 * h + 1], qp_ref[2 * h + 1])
            a1 = top_rows(s1)
            a2 = top_rows(s2)
            a2_lo = jnp.concatenate(a2[:8], axis=0)
            pieces = [a1[0] + jnp.concatenate(a2[:K], axis=0)]
            pieces += [a1[i] + a2_lo for i in range(1, 8)]
            pieces += [jnp.concatenate(a1[8:K], axis=0) + a2[0]]
            pieces += [a1[0] + a2[K], a1[K] + a2[0], jnp.full((6, TB), NEG_INF, F32)]
            cand = jnp.concatenate(pieces, axis=0)
            top = a1[0] + a2[0]
            z = jnp.zeros_like(top)
            m = top
            for _ in range(K):
                m = jnp.max(cand, axis=0, keepdims=True)
                z = z + jnp.exp(m - top)
                cand = jnp.where(cand >= m, NEG_INF, cand)
            tau = 0.5 * (m + jnp.max(cand, axis=0, keepdims=True))
            s2_s[h] = s2
            nb_s[h] = tau - s1
            e1_s[h] = jnp.exp(s1 - a1[0]) / z
            e2_s[h] = jnp.exp(s2 - a2[0])
            return carry

        lax.fori_loop(0, PEER_HEADS, head, 0)

    act_s[...] = lax.dot_general(u_ref[...], hn_ref[...], (((1,), (1,)), ((), ())),
                                 preferred_element_type=F32)
    n_i = ET // N_KEYS
    i0 = pl.multiple_of(e * n_i, n_i)
    for tc in range(TB // LANES):
        ts = slice(tc * LANES, (tc + 1) * LANES)
        nb_rows = [nb_s[h, pl.ds(i0, n_i), ts] for h in range(PEER_HEADS)]
        e1_rows = [e1_s[h, pl.ds(i0, n_i), ts] for h in range(PEER_HEADS)]
        for ii in range(n_i):
            G = jnp.zeros((N_KEYS, LANES), F32)
            for h in range(PEER_HEADS):
                nb_row = nb_rows[h][ii:ii + 1]
                e1_row = e1_rows[h][ii:ii + 1]
                G = G + jnp.where(s2_s[h, :, ts] >= nb_row, e2_s[h, :, ts], 0.0) * e1_row
            act = act_s[ii * N_KEYS:(ii + 1) * N_KEYS, ts]
            gelu = 0.5 * act * (1.0 + lax.erf(act * (1.0 / math.sqrt(2.0))))
            ga_s[ii * N_KEYS:(ii + 1) * N_KEYS, ts] = (G * gelu).astype(BF16)
    yt_s[...] += jnp.dot(vt_ref[...], ga_s[...], preferred_element_type=F32)

    @pl.when(e == pl.num_programs(1) - 1)
    def _():
        out_ref[...] = h_ref[...] + yt_s[...].T


def _peer(qp, keys, hn, h, u, vt, TB, ET):
    T = hn.shape[0]
    NQ = 2 * PEER_HEADS
    tok = lambda t, e: (t, 0)
    return pl.pallas_call(
        functools.partial(_peer_kernel, TB=TB, ET=ET),
        grid=(T // TB, N_EXPERTS // ET),
        in_specs=[pl.BlockSpec((NQ, TB, LANES), lambda t, e: (0, t, 0)),
                  pl.BlockSpec((NQ, N_KEYS, LANES), lambda t, e: (0, 0, 0)),
                  pl.BlockSpec((TB, D_MODEL), tok),
                  pl.BlockSpec((TB, D_MODEL), tok),
                  pl.BlockSpec((ET, D_MODEL), lambda t, e: (e, 0)),
                  pl.BlockSpec((D_MODEL, ET), lambda t, e: (0, e))],
        out_specs=pl.BlockSpec((TB, D_MODEL), tok),
        out_shape=jax.ShapeDtypeStruct((T, D_MODEL), F32),
        scratch_shapes=[pltpu.VMEM((PEER_HEADS, N_KEYS, TB), F32)] * 4
        + [pltpu.VMEM((ET, TB), F32), pltpu.VMEM((ET, TB), BF16), pltpu.VMEM((D_MODEL, TB), F32)],
        compiler_params=_cparams("parallel", "arbitrary"),
        name="peer",
    )(qp, keys, hn, h, u, vt)


def _pad_cols(w, n):
    return jnp.pad(w, ((0, 0), (0, n - w.shape[1])))


def _pad_rows(w, n):
    return jnp.pad(w, ((0, n - w.shape[0]), (0, 0)))


def _head_slabs(w, width):
    k = w.shape[0]
    w = w.reshape(k, MLA_HEADS, width)
    return jnp.pad(w, ((0, 0), (0, 0), (0, HEAD_SLAB - width))).reshape(k, MLA_HEADS * HEAD_SLAB)


def _lane_vec(nope, pe):
    v = jnp.zeros((LANES,), F32)
    if nope is not None:
        v = v.at[:QK_NOPE].set(nope)
    if pe is not None:
        v = v.at[QK_NOPE:QK_NOPE + QK_ROPE].set(pe)
    return v.reshape(1, LANES)


def kernel(x, positions, norm1_w, w_in, rw_mu, rw_w0, rw_w2, rw_a0, rw_a2, rw_g2, rw_k_k, rw_k_a,
           rw_r_k, rw_ln_w, rw_ln_b, rw_w_o, mla_q_norm_w, mla_w_uq, mla_kv_norm_w, mla_w_ukv,
           mla_qn_nope_w, mla_qn_pe_w, mla_kn_nope_w, mla_kn_pe_w, mla_w_o, w_out, norm2_w,
           peer_w_query, peer_sub_keys, peer_u, peer_v):
    B, S, D = x.shape
    T = B * S
    x2 = x.reshape(T, D)
    row = lambda v: v.reshape(1, -1).astype(F32)

    o_q = RW_SHIFT_WIDTH
    o_kv = o_q + Q_LORA
    o_pe = o_kv + KV_LORA
    o_g = o_pe + QK_ROPE
    seg3 = 3 * RW_WIDTH
    w_rw = jnp.concatenate([
        w_in[:, :seg3],
        _pad_cols(w_in[:, seg3:seg3 + DECAY_LORA], LANES),
        _pad_cols(w_in[:, seg3 + DECAY_LORA:seg3 + DECAY_LORA + AAA_LORA], LANES),
        _pad_cols(w_in[:, seg3 + DECAY_LORA + AAA_LORA:o_q], GATE_PAD)], axis=1)
    pe_slab = jnp.zeros((D, LANES), F32).at[:, QK_NOPE:QK_NOPE + QK_ROPE].set(w_in[:, o_pe:o_g])
    w_cat = jnp.concatenate([w_rw, w_in[:, o_q:o_pe], pe_slab, w_in[:, o_g:]], axis=1).astype(BF16)
    mu = jnp.concatenate([
        rw_mu[:seg3],
        jnp.pad(rw_mu[seg3:seg3 + DECAY_LORA], (0, LANES - DECAY_LORA)),
        jnp.pad(rw_mu[seg3 + DECAY_LORA:seg3 + DECAY_LORA + AAA_LORA], (0, LANES - AAA_LORA)),
        jnp.pad(rw_mu[seg3 + DECAY_LORA + AAA_LORA:], (0, GATE_PAD - GATE_LORA))]).reshape(1, RW_PAD)

    tm = min(256, S)
    p_rw, p_mla, g = _inproj(x2, row(norm1_w), w_cat, tm)

    hid = np.arange(RW_WIDTH) // RW_HEAD_DIM
    ones_blk = jnp.asarray((hid[:, None] == hid[None, :]).astype(np.float32)).astype(BF16)
    tp = min(512, S)
    r, lw, k2, v, kk, a, g_rw = _rwprep(
        p_rw, mu, row(rw_w0), _pad_rows(rw_w2, LANES), row(rw_a0), _pad_rows(rw_a2, LANES),
        _pad_rows(rw_g2, GATE_PAD), row(rw_k_k), row(rw_k_a), ones_blk, tp, S)
    tri = jnp.asarray(np.tril(np.ones((RW_CHUNK, RW_CHUNK), np.float32))).astype(BF16)
    y_rw = _rwscan(r, lw, k2, v, kk, a, row(rw_r_k), row(rw_ln_w), row(rw_ln_b), tri, B, S)

    dq = QK_NOPE + QK_ROPE
    wuq = _head_slabs(mla_w_uq, dq).astype(BF16)
    ukv = mla_w_ukv.reshape(KV_LORA, MLA_HEADS, QK_NOPE + V_DIM)
    wuk = _head_slabs(ukv[:, :, :QK_NOPE].reshape(KV_LORA, -1), QK_NOPE).astype(BF16)
    wuv = _head_slabs(ukv[:, :, QK_NOPE:].reshape(KV_LORA, -1), V_DIM).astype(BF16)
    lane = np.arange(LANES)
    in_nope = lane < QK_NOPE
    in_pe = (lane >= QK_NOPE) & (lane < dq)
    seg = (in_nope[:, None] & in_nope[None, :]) / QK_NOPE + (in_pe[:, None] & in_pe[None, :]) / QK_ROPE
    half = QK_ROPE // 2
    inv_freq = ROPE_BASE ** (-jnp.arange(half, dtype=F32) / half)
    freq = jnp.zeros((LANES,), F32).at[QK_NOPE:dq].set(jnp.concatenate([inv_freq, inv_freq]))
    q, k, vv = _mlaprep(
        p_mla, positions.reshape(T, 1).astype(jnp.int32), row(mla_q_norm_w), wuq,
        row(mla_kv_norm_w), wuk, wuv, jnp.asarray(seg.astype(np.float32)).astype(BF16),
        _lane_vec(mla_qn_nope_w, mla_qn_pe_w), _lane_vec(mla_kn_nope_w, None),
        _lane_vec(None, mla_kn_pe_w), freq.reshape(1, LANES), tp, B, S)
    o_mla = _attention(q, k, vv, min(512, S)).reshape(T, MLA_HEADS * HEAD_SLAB)

    wmo = jnp.pad(mla_w_o.reshape(MLA_HEADS, V_DIM, D), ((0, 0), (0, HEAD_SLAB - V_DIM), (0, 0)))
    wmo = wmo.reshape(MLA_HEADS * HEAD_SLAB, D).astype(BF16)
    h, hn, qp = _merge(x2, y_rw, g_rw, o_mla, g, rw_w_o.astype(BF16), wmo, w_out.astype(BF16),
                       row(norm2_w), peer_w_query.astype(BF16), tm)

    keys = peer_sub_keys.reshape(2 * PEER_HEADS, N_KEYS, D_QUERY // 2).astype(BF16)
    out = _peer(qp, keys, hn, h, peer_u.astype(BF16), peer_v.astype(BF16).T, min(512, T), 1024)
    return out.reshape(B, S, D)
```
